```python
import math
import jax, jax.numpy as jnp
from jax import lax
import numpy as np

D_MODEL = 1024
BATCH = 8
SEQ = 4096
DEPTH = 2

PLE_DIM = 256
N_EVEN = (DEPTH + 1) // 2
N_ODD = DEPTH // 2
S5_WIDTH = D_MODEL // 2
S5_GROUP = 16
S5_GROUPS = S5_WIDTH // S5_GROUP
S5_STATE = 64
SB_HEAD_DIM = 64
SB_HEADS = (D_MODEL // 2) // SB_HEAD_DIM
SB_WIDTH = SB_HEADS * SB_HEAD_DIM
MIX_WIDTH = S5_WIDTH + SB_WIDTH
IN_WIDTH = S5_WIDTH + 3 * SB_WIDTH
Q_BLOCK = 128
POOL_WINDOWS = (2, 4, 8, 16)
POOL_GROUP = D_MODEL // len(POOL_WINDOWS)
D_FF = 4 * D_MODEL
EPS = 1e-6
DT_MIN = 1e-3
DT_MAX = 1e-1

kernel_name = "hybrid_s5_stickbreak_pool_trunk"


def rms_norm(x, gain):
    xf = x.astype(jnp.float32)
    y = xf * lax.rsqrt(jnp.mean(xf * xf, axis=-1, keepdims=True) + EPS)
    return (y * gain.astype(jnp.float32)).astype(x.dtype)


def _cmul(ar, ai, br, bi):
    return ar * br - ai * bi, ar * bi + ai * br


def _s5_combine(earlier, later):
    a1r, a1i, b1r, b1i = earlier
    a2r, a2i, b2r, b2i = later
    ar, ai = _cmul(a2r, a2i, a1r, a1i)
    cr, ci = _cmul(a2r, a2i, b1r, b1i)
    return ar, ai, cr + b2r, ci + b2i


def s5_mixer(u, lam_re, lam_im, log_dt, b_re, b_im, c_re, c_im, d, w_glu):
    bsz, seqlen, _ = u.shape
    f32 = jnp.float32
    uf = u.astype(f32)
    ug = uf.reshape(bsz, seqlen, S5_GROUPS, S5_GROUP)
    lr = lam_re.astype(f32)
    li = lam_im.astype(f32)
    dt = jnp.exp(log_dt.astype(f32))[:, None]
    mag = jnp.exp(lr * dt)
    abar_r = mag * jnp.cos(li * dt)
    abar_i = mag * jnp.sin(li * dt)
    den = lr * lr + li * li
    nr = abar_r - 1.0
    ni = abar_i
    fr = (nr * lr + ni * li) / den
    fi = (ni * lr - nr * li) / den
    br = b_re.astype(f32)
    bi = b_im.astype(f32)
    bbar_r = fr[..., None] * br - fi[..., None] * bi
    bbar_i = fr[..., None] * bi + fi[..., None] * br
    bu_r = jnp.einsum('blgh,gph->blgp', ug, bbar_r)
    bu_i = jnp.einsum('blgh,gph->blgp', ug, bbar_i)
    a_r = jnp.broadcast_to(abar_r, bu_r.shape)
    a_i = jnp.broadcast_to(abar_i, bu_i.shape)
    _, _, x_r, x_i = lax.associative_scan(_s5_combine, (a_r, a_i, bu_r, bu_i), axis=1)
    y = (jnp.einsum('blgp,ghp->blgh', x_r, c_re.astype(f32))
         - jnp.einsum('blgp,ghp->blgh', x_i, c_im.astype(f32)))
    y = y.reshape(bsz, seqlen, S5_WIDTH) + d.astype(f32) * uf
    y = jax.nn.gelu(y)
    y = y * jax.nn.sigmoid(y @ w_glu.astype(f32))
    return y.astype(u.dtype)


def stick_breaking_attention(q, k, v):
    bsz, nh, seqlen, dh = q.shape
    n_blocks = seqlen // Q_BLOCK
    scale = dh ** -0.5
    qf = q.astype(jnp.float32)
    kf = k.astype(jnp.float32)
    q_blocks = qf.reshape(bsz, nh, n_blocks, Q_BLOCK, dh).transpose(2, 0, 1, 3, 4)
    key_pos = jnp.arange(seqlen)

    def one_block(args):
        qb, blk = args
        z = jnp.einsum('bhqd,bhkd->bhqk', qb, kf) * scale
        q_pos = blk * Q_BLOCK + jnp.arange(Q_BLOCK)
        causal = key_pos[None, :] < q_pos[:, None]
        log_beta = jax.nn.log_sigmoid(z)
        log_1m_beta = jnp.where(causal, jax.nn.log_sigmoid(-z), 0.0)
        tail = lax.cumsum(log_1m_beta, axis=3, reverse=True) - log_1m_beta
        w = jnp.where(causal, jnp.exp(log_beta + tail), 0.0)
        return jnp.einsum('bhqk,bhkd->bhqd', w.astype(v.dtype), v)

    out = lax.map(one_block, (q_blocks, jnp.arange(n_blocks)))
    return out.transpose(1, 2, 0, 3, 4).reshape(bsz, nh, seqlen, dh)


def causal_window_mean(x, window):
    seqlen = x.shape[1]
    cs = jnp.cumsum(x, axis=1)
    shifted = jnp.pad(cs, ((0, 0), (window, 0), (0, 0)))[:, :seqlen]
    count = jnp.minimum(jnp.arange(seqlen) + 1, window).astype(jnp.float32)
    return (cs - shifted) / count[None, :, None]


def pool_mixer(h, pool_w, pool_scale):
    bsz, seqlen, _ = h.shape
    hf = h.astype(jnp.float32)
    outs = []
    for g, window in enumerate(POOL_WINDOWS):
        xg = hf[..., g * POOL_GROUP:(g + 1) * POOL_GROUP]
        outs.append(causal_window_mean(xg, window) - xg)
    y = jnp.stack(outs, axis=2)
    y = jnp.einsum('blgc,gcd->blgd', y, pool_w.astype(jnp.float32))
    y = y.reshape(bsz, seqlen, D_MODEL) * pool_scale.astype(jnp.float32)
    return y.astype(h.dtype)


def even_mixer(h, ln, w_in, lam_re, lam_im, log_dt, b_re, b_im, c_re, c_im, d, w_glu,
               q_gain, k_gain, w_out):
    bsz, seqlen, _ = h.shape
    hn = rms_norm(h, ln)
    proj = hn @ w_in
    u = proj[..., :S5_WIDTH]
    q, k, v = jnp.split(proj[..., S5_WIDTH:], 3, axis=-1)
    to_heads = lambda t: t.reshape(bsz, seqlen, SB_HEADS, SB_HEAD_DIM).transpose(0, 2, 1, 3)
    q = rms_norm(to_heads(q), q_gain)
    k = rms_norm(to_heads(k), k_gain)
    v = to_heads(v)
    sb = stick_breaking_attention(q, k, v).transpose(0, 2, 1, 3).reshape(bsz, seqlen, SB_WIDTH)
    s5 = s5_mixer(u, lam_re, lam_im, log_dt, b_re, b_im, c_re, c_im, d, w_glu)
    mixed = jnp.concatenate([s5, sb.astype(s5.dtype)], axis=-1)
    return mixed @ w_out


def setup_inputs(seed: int = 0) -> dict:
    key = jax.random.key(seed)
    ks = jax.random.split(key, 32)
    f32 = jnp.float32
    nrm = lambda k, shape, scale: scale * jax.random.normal(k, shape, f32)
    gain = lambda k, shape: 1.0 + 0.01 * jax.random.normal(k, shape, f32)
    n = jnp.arange(S5_STATE, dtype=f32)
    return {
        'x': nrm(ks[0], (BATCH, SEQ, D_MODEL), 1.0),
        'p': nrm(ks[1], (DEPTH, BATCH, SEQ, PLE_DIM), 1.0),
        'ln_mix_even': gain(ks[2], (N_EVEN, D_MODEL)),
        'w_in_even': nrm(ks[3], (N_EVEN, D_MODEL, IN_WIDTH), D_MODEL ** -0.5),
        's5_lambda_re': -0.5 + nrm(ks[4], (N_EVEN, S5_GROUPS, S5_STATE), 0.01),
        's5_lambda_im': jnp.pi * n + nrm(ks[5], (N_EVEN, S5_GROUPS, S5_STATE), 0.01),
        's5_log_dt': jax.random.uniform(ks[6], (N_EVEN, S5_GROUPS), f32,
                                        math.log(DT_MIN), math.log(DT_MAX)),
        's5_b_re': nrm(ks[7], (N_EVEN, S5_GROUPS, S5_STATE, S5_GROUP), (2.0 * S5_GROUP) ** -0.5),
        's5_b_im': nrm(ks[8], (N_EVEN, S5_GROUPS, S5_STATE, S5_GROUP), (2.0 * S5_GROUP) ** -0.5),
        's5_c_re': nrm(ks[9], (N_EVEN, S5_GROUPS, S5_GROUP, S5_STATE), S5_STATE ** -0.5),
        's5_c_im': nrm(ks[10], (N_EVEN, S5_GROUPS, S5_GROUP, S5_STATE), S5_STATE ** -0.5),
        's5_d': nrm(ks[11], (N_EVEN, S5_WIDTH), 1.0),
        's5_w_glu': nrm(ks[12], (N_EVEN, S5_WIDTH, S5_WIDTH), S5_WIDTH ** -0.5),
        'sb_q_gain': gain(ks[13], (N_EVEN, SB_HEAD_DIM)),
        'sb_k_gain': gain(ks[14], (N_EVEN, SB_HEAD_DIM)),
        'w_out_even': nrm(ks[15], (N_EVEN, MIX_WIDTH, D_MODEL), MIX_WIDTH ** -0.5),
        'ln_mix_odd': gain(ks[16], (N_ODD, D_MODEL)),
        'pool_w': nrm(ks[17], (N_ODD, len(POOL_WINDOWS), POOL_GROUP, POOL_GROUP), POOL_GROUP ** -0.5),
        'pool_scale': gain(ks[18], (N_ODD, D_MODEL)),
        'ln_mlp': gain(ks[19], (DEPTH, D_MODEL)),
        'w_mlp_up': nrm(ks[20], (DEPTH, D_MODEL, D_FF), D_MODEL ** -0.5),
        'w_mlp_down': nrm(ks[21], (DEPTH, D_FF, D_MODEL), 0.5 * D_FF ** -0.5),
        'ln_ple': gain(ks[22], (DEPTH, D_MODEL)),
        'w_ple_gate': nrm(ks[23], (DEPTH, D_MODEL, D_MODEL), D_MODEL ** -0.5),
        'w_ple_up': nrm(ks[24], (DEPTH, PLE_DIM, D_MODEL), PLE_DIM ** -0.5),
    }


def reference(x, p, ln_mix_even, w_in_even, s5_lambda_re, s5_lambda_im, s5_log_dt,
              s5_b_re, s5_b_im, s5_c_re, s5_c_im, s5_d, s5_w_glu, sb_q_gain, sb_k_gain,
              w_out_even, ln_mix_odd, pool_w, pool_scale, ln_mlp, w_mlp_up, w_mlp_down,
              ln_ple, w_ple_gate, w_ple_up):
    h = x
    for i in range(DEPTH):
        j = i // 2
        if i % 2 == 0:
            h = h + even_mixer(h, ln_mix_even[j], w_in_even[j], s5_lambda_re[j], s5_lambda_im[j],
                               s5_log_dt[j], s5_b_re[j], s5_b_im[j], s5_c_re[j], s5_c_im[j],
                               s5_d[j], s5_w_glu[j], sb_q_gain[j], sb_k_gain[j], w_out_even[j])
        else:
            h = h + pool_mixer(rms_norm(h, ln_mix_odd[j]), pool_w[j], pool_scale[j])
        hn = rms_norm(h, ln_mlp[i])
        h = h + jnp.square(jax.nn.relu(hn @ w_mlp_up[i])) @ w_mlp_down[i]
        gate = jax.nn.sigmoid(rms_norm(h, ln_ple[i]) @ w_ple_gate[i])
        h = h + (p[i] @ w_ple_up[i]) * gate
    return h
```

```python
import functools
import math

import jax
import jax.numpy as jnp
from jax import lax
from jax.experimental import pallas as pl
from jax.experimental.pallas import tpu as pltpu

F32 = jnp.float32
BF16 = jnp.bfloat16

D_MODEL = 1024
S5_WIDTH = 512
S5_GROUP = 16
S5_GROUPS = 32
S5_STATE = 64
SB_HEAD_DIM = 64
SB_WIDTH = 512
IN_WIDTH = S5_WIDTH + 3 * SB_WIDTH
POOL_WINDOWS = (2, 4, 8, 16)
POOL_GROUP = 256
POOL_HALO = 16
D_FF = 4 * D_MODEL
PLE_DIM = 256
EPS = 1e-6

LANES = 128
SUBLANES = 8
STATE_LANES = 2 * S5_GROUPS * S5_STATE
STATE_HALF = STATE_LANES // 2
COL_TILE = 2 * LANES

VMEM_LIMIT = 56 * 1024 * 1024


def _const_spec(shape):
    nd = len(shape)
    return pl.BlockSpec(shape, lambda *_: (0,) * nd, pipeline_mode=pl.Buffered(1))


def _rms(x, gain):
    ms = jnp.mean(x * x, axis=-1, keepdims=True)
    return x * lax.rsqrt(ms + EPS) * gain


def _inproj_kernel(x_ref, ln_ref, w_ref, qg_ref, kg_ref, hm_ref, u_ref, q_ref, k_ref, v_ref):
    hn = _rms(x_ref[0], ln_ref[...]).astype(BF16)
    proj = jnp.dot(hn, w_ref[...], preferred_element_type=F32)
    u_ref[...] = proj[:, :S5_WIDTH]
    hm = hm_ref[...]

    def head_norm(t, gain):
        ms = jnp.dot((t * t).astype(BF16), hm, preferred_element_type=F32)
        return t * lax.rsqrt(ms + EPS) * gain

    q = proj[:, S5_WIDTH:S5_WIDTH + SB_WIDTH]
    k = proj[:, S5_WIDTH + SB_WIDTH:S5_WIDTH + 2 * SB_WIDTH]
    q_ref[0] = (head_norm(q, qg_ref[...]) * (SB_HEAD_DIM ** -0.5)).astype(BF16)
    k_ref[0] = head_norm(k, kg_ref[...]).astype(BF16)
    v_ref[0] = proj[:, S5_WIDTH + 2 * SB_WIDTH:].astype(BF16)


def _inproj(x, ln, w_in, q_gain, k_gain, tl=512):
    bsz, seqlen, _ = x.shape
    heads = SB_WIDTH // SB_HEAD_DIM
    hm = jnp.kron(jnp.eye(heads, dtype=F32), jnp.full((SB_HEAD_DIM, SB_HEAD_DIM), 1.0 / SB_HEAD_DIM, F32)).astype(BF16)
    qg = jnp.tile(q_gain.astype(F32), heads)[None, :]
    kg = jnp.tile(k_gain.astype(F32), heads)[None, :]
    act = lambda: pl.BlockSpec((1, tl, SB_WIDTH), lambda b, i: (b, i, 0))
    return pl.pallas_call(
        _inproj_kernel,
        grid=(bsz, seqlen // tl),
        in_specs=[
            pl.BlockSpec((1, tl, D_MODEL), lambda b, i: (b, i, 0)),
            _const_spec((1, D_MODEL)),
            _const_spec((D_MODEL, IN_WIDTH)),
            _const_spec((1, SB_WIDTH)),
            _const_spec((1, SB_WIDTH)),
            _const_spec((SB_WIDTH, SB_WIDTH)),
        ],
        out_specs=[
            pl.BlockSpec((tl, S5_WIDTH), lambda b, i: (i, b)),
            act(), act(), act(),
        ],
        out_shape=[
            jax.ShapeDtypeStruct((seqlen, bsz * S5_WIDTH), F32),
            jax.ShapeDtypeStruct((bsz, seqlen, SB_WIDTH), BF16),
            jax.ShapeDtypeStruct((bsz, seqlen, SB_WIDTH), BF16),
            jax.ShapeDtypeStruct((bsz, seqlen, SB_WIDTH), BF16),
        ],
        compiler_params=pltpu.CompilerParams(
            dimension_semantics=("parallel", "parallel"), vmem_limit_bytes=VMEM_LIMIT),
        name="inproj",
    )(x, ln.astype(F32)[None, :], w_in.astype(BF16), qg, kg, hm)


S5_TILES_PER_PASS = 4


def _s5_kernel(u_ref, bw_ref, cw_ref, ar_ref, ai_ref, d_ref, wglu_ref, y_ref, st_ref, x_ref, *, steps):
    @pl.when(pl.program_id(0) == 0)
    def _():
        st_ref[...] = jnp.zeros_like(st_ref)

    u = u_ref[...]
    ub = u.astype(BF16)
    half_ch = S5_WIDTH // 2
    for kt in range(2):
        x_ref[:, kt * STATE_HALF:(kt + 1) * STATE_HALF] = jnp.dot(
            ub[:, kt * half_ch:(kt + 1) * half_ch], bw_ref[kt], preferred_element_type=F32)

    n_tiles = STATE_LANES // COL_TILE
    for t0 in range(0, n_tiles, S5_TILES_PER_PASS):
        tiles = range(t0, t0 + S5_TILES_PER_PASS)
        ar = [jnp.broadcast_to(ar_ref[:, c * LANES:(c + 1) * LANES], (SUBLANES, LANES)) for c in tiles]
        ai = [jnp.broadcast_to(ai_ref[:, c * LANES:(c + 1) * LANES], (SUBLANES, LANES)) for c in tiles]
        init = tuple(st_ref[:, c * COL_TILE + r * LANES:c * COL_TILE + (r + 1) * LANES]
                     for c in tiles for r in range(2))

        def body(t, carry, tiles=tiles, ar=ar, ai=ai):
            r0 = pl.multiple_of(t * SUBLANES, SUBLANES)
            out = []
            for n, c in enumerate(tiles):
                re_sl = slice(c * COL_TILE, c * COL_TILE + LANES)
                im_sl = slice(c * COL_TILE + LANES, (c + 1) * COL_TILE)
                xr, xi = carry[2 * n], carry[2 * n + 1]
                nxr = ar[n] * xr - ai[n] * xi + x_ref[pl.ds(r0, SUBLANES), re_sl]
                nxi = ar[n] * xi + ai[n] * xr + x_ref[pl.ds(r0, SUBLANES), im_sl]
                x_ref[pl.ds(r0, SUBLANES), re_sl] = nxr
                x_ref[pl.ds(r0, SUBLANES), im_sl] = nxi
                out += [nxr, nxi]
            return tuple(out)

        final = lax.fori_loop(0, steps, body, init, unroll=4)
        for n, c in enumerate(tiles):
            st_ref[:, c * COL_TILE:c * COL_TILE + LANES] = final[2 * n]
            st_ref[:, c * COL_TILE + LANES:(c + 1) * COL_TILE] = final[2 * n + 1]

    xb = x_ref[...].astype(BF16)
    y = jnp.concatenate(
        [jnp.dot(xb[:, kt * STATE_HALF:(kt + 1) * STATE_HALF], cw_ref[kt], preferred_element_type=F32)
         for kt in range(2)], axis=1)
    y = y + d_ref[...] * u
    y = jax.nn.gelu(y)
    g = jnp.dot(y.astype(BF16), wglu_ref[...], preferred_element_type=F32)
    y_ref[...] = (y * jax.nn.sigmoid(g)).astype(BF16)


def _s5_params(lam_re, lam_im, log_dt, b_re, b_im, c_re, c_im):
    lr = lam_re.astype(F32)
    li = lam_im.astype(F32)
    dt = jnp.exp(log_dt.astype(F32))[:, None]
    mag = jnp.exp(lr * dt)
    abar_r = mag * jnp.cos(li * dt)
    abar_i = mag * jnp.sin(li * dt)
    den = lr * lr + li * li
    nr = abar_r - 1.0
    ni = abar_i
    fr = (nr * lr + ni * li) / den
    fi = (ni * lr - nr * li) / den
    br = b_re.astype(F32)
    bi = b_im.astype(F32)
    bbar_r = fr[..., None] * br - fi[..., None] * bi
    bbar_i = fr[..., None] * bi + fi[..., None] * br
    g = jnp.arange(S5_GROUPS)
    pair = jax.nn.one_hot(g // 2, S5_GROUPS // 2, dtype=F32)
    par = jax.nn.one_hot(g % 2, 2, dtype=F32)
    bblk = jnp.stack([bbar_r.transpose(0, 2, 1), bbar_i.transpose(0, 2, 1)], axis=2)
    bfull = jnp.einsum('ghrp,gc,gq->ghcrqp', bblk, pair, par).reshape(S5_WIDTH, STATE_LANES)
    cblk = jnp.stack([c_re.astype(F32), -c_im.astype(F32)], axis=1)
    cfull = jnp.einsum('grhp,gc,gq->crqpgh', cblk, pair, par).reshape(STATE_LANES, S5_WIDTH)
    half_ch = S5_WIDTH // 2
    bw = jnp.stack([bfull[kt * half_ch:(kt + 1) * half_ch, kt * STATE_HALF:(kt + 1) * STATE_HALF]
                    for kt in range(2)]).astype(BF16)
    cw = jnp.stack([cfull[kt * STATE_HALF:(kt + 1) * STATE_HALF, kt * half_ch:(kt + 1) * half_ch]
                    for kt in range(2)]).astype(BF16)
    a_r = abar_r.reshape(1, S5_GROUPS * S5_STATE)
    a_i = abar_i.reshape(1, S5_GROUPS * S5_STATE)
    return bw, cw, a_r, a_i


def _s5(u_tb, bsz, lam_re, lam_im, log_dt, b_re, b_im, c_re, c_im, d, w_glu, steps=32):
    rows_total = u_tb.shape[0]
    assert bsz == SUBLANES
    rows = steps * SUBLANES
    bw, cw, a_r, a_i = _s5_params(lam_re, lam_im, log_dt, b_re, b_im, c_re, c_im)
    return pl.pallas_call(
        functools.partial(_s5_kernel, steps=steps),
        grid=(rows_total // rows,),
        in_specs=[
            pl.BlockSpec((rows, S5_WIDTH), lambda c: (c, 0)),
            _const_spec(bw.shape), _const_spec(cw.shape),
            _const_spec(a_r.shape), _const_spec(a_i.shape),
            _const_spec((1, S5_WIDTH)), _const_spec((S5_WIDTH, S5_WIDTH)),
        ],
        out_specs=pl.BlockSpec((rows, S5_WIDTH), lambda c: (c, 0)),
        out_shape=jax.ShapeDtypeStruct((rows_total, S5_WIDTH), BF16),
        scratch_shapes=[pltpu.VMEM((SUBLANES, STATE_LANES), F32), pltpu.VMEM((rows, STATE_LANES), F32)],
        compiler_params=pltpu.CompilerParams(
            dimension_semantics=("arbitrary",), vmem_limit_bytes=VMEM_LIMIT),
        name="s5",
    )(u_tb, bw, cw, a_r, a_i, d.astype(F32)[None, :], w_glu.astype(BF16))


def _attn_kernel(q_ref, k_ref, v_ref, o_ref, *, blk):
    i = pl.program_id(2)
    q = q_ref[0]
    lane = lax.broadcasted_iota(jnp.int32, q.shape, 1)
    zero = jnp.zeros_like(q)
    qh = (jnp.where(lane < SB_HEAD_DIM, q, zero), jnp.where(lane >= SB_HEAD_DIM, q, zero))
    row = lax.broadcasted_iota(jnp.int32, (blk, blk), 0)
    col = lax.broadcasted_iota(jnp.int32, (blk, blk), 1)
    later = (row > col).astype(BF16)
    causal = col < row

    def tile(j, carry, diag):
        k0 = pl.multiple_of(j * blk, blk)
        kb = k_ref[0, pl.ds(k0, blk), :]
        vb = v_ref[0, pl.ds(k0, blk), :]
        out = []
        for h in range(2):
            acc, run = carry[2 * h], carry[2 * h + 1]
            z = lax.dot_general(qh[h], kb, (((1,), (1,)), ((), ())), preferred_element_type=F32)
            sp = jnp.log(1.0 + jnp.exp(-jnp.abs(z)))
            log_beta = jnp.minimum(z, 0.0) - sp
            log_1m = log_beta - z
            if diag:
                log_1m = jnp.where(causal, log_1m, 0.0)
            hi = log_1m.astype(BF16)
            lo = (log_1m - hi.astype(F32)).astype(BF16)
            tail = (jnp.dot(hi, later, preferred_element_type=F32)
                    + jnp.dot(lo, later, preferred_element_type=F32))
            w = jnp.exp(log_beta + tail + run)
            if diag:
                w = jnp.where(causal, w, 0.0)
            acc = acc + jnp.dot(w.astype(BF16), vb, preferred_element_type=F32)
            run = run + jnp.sum(log_1m, axis=1, keepdims=True)
            out += [acc, run]
        return tuple(out)

    acc0 = jnp.zeros((blk, LANES), F32)
    run0 = jnp.zeros((blk, 1), F32)
    carry = tile(i, (acc0, run0, acc0, run0), True)
    carry = lax.fori_loop(0, i, lambda jj, c: tile(i - 1 - jj, c, False), carry)
    o_ref[0] = jnp.where(lane < SB_HEAD_DIM, carry[0], carry[2]).astype(BF16)


def _sbattn(q, k, v, blk=256):
    bsz, seqlen, _ = q.shape
    pairs = SB_WIDTH // LANES
    qspec = pl.BlockSpec((1, blk, LANES), lambda b, p, i: (b, i, p))
    kvspec = pl.BlockSpec((1, seqlen, LANES), lambda b, p, i: (b, 0, p))
    return pl.pallas_call(
        functools.partial(_attn_kernel, blk=blk),
        grid=(bsz, pairs, seqlen // blk),
        in_specs=[qspec, kvspec, kvspec],
        out_specs=qspec,
        out_shape=jax.ShapeDtypeStruct((bsz, seqlen, SB_WIDTH), BF16),
        compiler_params=pltpu.CompilerParams(
            dimension_semantics=("parallel", "parallel", "arbitrary"), vmem_limit_bytes=VMEM_LIMIT),
        name="sbattn",
    )(q, k, v)


FF_CHUNK = 1024


def _mlp_ple(h, p_ref, lnm_ref, wup_ref, wdn_ref, lnp_ref, wg_ref, wpu_ref, o_ref):
    hn = _rms(h, lnm_ref[...]).astype(BF16)
    acc = h
    for c in range(0, D_FF, FF_CHUNK):
        a = jnp.dot(hn, wup_ref[:, c:c + FF_CHUNK], preferred_element_type=F32)
        a = jnp.square(jnp.maximum(a, 0.0)).astype(BF16)
        acc = acc + jnp.dot(a, wdn_ref[c:c + FF_CHUNK, :], preferred_element_type=F32)
    gate = jax.nn.sigmoid(jnp.dot(_rms(acc, lnp_ref[...]).astype(BF16), wg_ref[...], preferred_element_type=F32))
    pe = jnp.dot(p_ref[0].astype(BF16), wpu_ref[...], preferred_element_type=F32)
    o_ref[0] = acc + pe * gate


def _layer_even_kernel(x_ref, s5_ref, sb_ref, wout_ref, p_ref, lnm_ref, wup_ref, wdn_ref, lnp_ref, wg_ref,
                       wpu_ref, o_ref):
    mixed = (jnp.dot(s5_ref[...], wout_ref[:S5_WIDTH, :], preferred_element_type=F32)
             + jnp.dot(sb_ref[0], wout_ref[S5_WIDTH:, :], preferred_element_type=F32))
    _mlp_ple(x_ref[0] + mixed, p_ref, lnm_ref, wup_ref, wdn_ref, lnp_ref, wg_ref, wpu_ref, o_ref)


def _layer_odd_kernel(h_ref, halo_ref, lno_ref, pw_ref, ps_ref, p_ref, lnm_ref, wup_ref, wdn_ref, lnp_ref,
                      wg_ref, wpu_ref, o_ref, *, tm):
    i = pl.program_id(1)
    h = h_ref[0]
    hn = _rms(h, lno_ref[...])
    halo = _rms(halo_ref[0], lno_ref[...]) * (i > 0).astype(F32)
    ext = jnp.concatenate([halo, hn], axis=0)
    t = lax.broadcasted_iota(jnp.int32, (tm, 1), 0) + i * tm
    outs = []
    for g, window in enumerate(POOL_WINDOWS):
        sl = slice(g * POOL_GROUP, (g + 1) * POOL_GROUP)
        s = ext[:, sl]
        span = 1
        while span < window:
            s = s + jnp.concatenate([jnp.zeros((span, POOL_GROUP), F32), s[:-span]], axis=0)
            span *= 2
        count = jnp.minimum(t + 1, window).astype(F32)
        y = s[POOL_HALO:] / count - hn[:, sl]
        outs.append(jnp.dot(y.astype(BF16), pw_ref[g], preferred_element_type=F32))
    mixed = jnp.concatenate(outs, axis=1) * ps_ref[...]
    _mlp_ple(h + mixed, p_ref, lnm_ref, wup_ref, wdn_ref, lnp_ref, wg_ref, wpu_ref, o_ref)


def _tail_specs(tm):
    return [
        pl.BlockSpec((1, tm, PLE_DIM), lambda b, i: (b, i, 0)),
        _const_spec((1, D_MODEL)), _const_spec((D_MODEL, D_FF)), _const_spec((D_FF, D_MODEL)),
        _const_spec((1, D_MODEL)), _const_spec((D_MODEL, D_MODEL)), _const_spec((PLE_DIM, D_MODEL)),
    ]


def _tail_args(p_i, ln_mlp, w_up, w_down, ln_ple, w_gate, w_ple_up):
    return (p_i, ln_mlp.astype(F32)[None, :], w_up.astype(BF16), w_down.astype(BF16),
            ln_ple.astype(F32)[None, :], w_gate.astype(BF16), w_ple_up.astype(BF16))


def _layer_even(x, s5_tb, sb, w_out, tail, tm=512):
    bsz, seqlen, _ = x.shape
    row = lambda w: pl.BlockSpec((1, tm, w), lambda b, i: (b, i, 0))
    return pl.pallas_call(
        _layer_even_kernel,
        grid=(bsz, seqlen // tm),
        in_specs=[row(D_MODEL),
                  pl.BlockSpec((tm, S5_WIDTH), lambda b, i: (i, b)),
                  row(SB_WIDTH),
                  _const_spec((D_MODEL, D_MODEL))] + _tail_specs(tm),
        out_specs=row(D_MODEL),
        out_shape=jax.ShapeDtypeStruct(x.shape, F32),
        compiler_params=pltpu.CompilerParams(
            dimension_semantics=("parallel", "parallel"), vmem_limit_bytes=VMEM_LIMIT),
        name="layer_even",
    )(x, s5_tb, sb, w_out.astype(BF16), *tail)


def _layer_odd(h, ln_odd, pool_w, pool_scale, tail, tm=512):
    bsz, seqlen, _ = h.shape
    row = lambda w: pl.BlockSpec((1, tm, w), lambda b, i: (b, i, 0))
    halo_blocks = tm // POOL_HALO
    return pl.pallas_call(
        functools.partial(_layer_odd_kernel, tm=tm),
        grid=(bsz, seqlen // tm),
        in_specs=[row(D_MODEL),
                  pl.BlockSpec((1, POOL_HALO, D_MODEL), lambda b, i: (b, jnp.maximum(i * halo_blocks - 1, 0), 0)),
                  _const_spec((1, D_MODEL)),
                  _const_spec((len(POOL_WINDOWS), POOL_GROUP, POOL_GROUP)),
                  _const_spec((1, D_MODEL))] + _tail_specs(tm),
        out_specs=row(D_MODEL),
        out_shape=jax.ShapeDtypeStruct(h.shape, F32),
        compiler_params=pltpu.CompilerParams(
            dimension_semantics=("parallel", "parallel"), vmem_limit_bytes=VMEM_LIMIT),
        name="layer_odd",
    )(h, h, ln_odd.astype(F32)[None, :], pool_w.astype(BF16), pool_scale.astype(F32)[None, :], *tail)


def kernel(x, p, ln_mix_even, w_in_even, s5_lambda_re, s5_lambda_im, s5_log_dt, s5_b_re, s5_b_im, s5_c_re,
           s5_c_im, s5_d, s5_w_glu, sb_q_gain, sb_k_gain, w_out_even, ln_mix_odd, pool_w, pool_scale, ln_mlp,
           w_mlp_up, w_mlp_down, ln_ple, w_ple_gate, w_ple_up):
    bsz, seqlen, _ = x.shape
    h = x
    for i in range(p.shape[0]):
        j = i // 2
        tail = _tail_args(p[i], ln_mlp[i], w_mlp_up[i], w_mlp_down[i], ln_ple[i], w_ple_gate[i], w_ple_up[i])
        if i % 2 == 0:
            u, q, k, v = _inproj(h, ln_mix_even[j], w_in_even[j], sb_q_gain[j], sb_k_gain[j])
            u_tb = u.reshape(seqlen * bsz, S5_WIDTH)
            s5_tb = _s5(u_tb, bsz, s5_lambda_re[j], s5_lambda_im[j], s5_log_dt[j], s5_b_re[j], s5_b_im[j],
                        s5_c_re[j], s5_c_im[j], s5_d[j], s5_w_glu[j])
            sb = _sbattn(q, k, v)
            h = _layer_even(h, s5_tb.reshape(seqlen, bsz * S5_WIDTH), sb, w_out_even[j], tail)
        else:
            h = _layer_odd(h, ln_mix_odd[j], pool_w[j], pool_scale[j], tail)
    return h
```

```python
import functools
import math

import jax
import jax.numpy as jnp
from jax import lax
from jax.experimental import pallas as pl
from jax.experimental.pallas import tpu as pltpu

F32 = jnp.float32
BF16 = jnp.bfloat16

D_MODEL = 1024
S5_WIDTH = 512
S5_GROUP = 16
S5_GROUPS = 32
S5_STATE = 64
SB_HEAD_DIM = 64
SB_WIDTH = 512
IN_WIDTH = S5_WIDTH + 3 * SB_WIDTH
POOL_WINDOWS = (2, 4, 8, 16)
POOL_GROUP = 256
POOL_HALO = 16
D_FF = 4 * D_MODEL
PLE_DIM = 256
EPS = 1e-6
LOG2_E = math.log2(math.e)
ZERO_WEIGHT_LOG2 = -150.0

LANES = 128
SUBLANES = 8
STATE_LANES = 2 * S5_GROUPS * S5_STATE
STATE_HALF = STATE_LANES // 2
COL_TILE = 2 * LANES

VMEM_LIMIT = 56 * 1024 * 1024


def _const_spec(shape):
    nd = len(shape)
    return pl.BlockSpec(shape, lambda *_: (0,) * nd, pipeline_mode=pl.Buffered(1))


def _rms(x, gain):
    ms = jnp.mean(x * x, axis=-1, keepdims=True)
    return x * lax.rsqrt(ms + EPS) * gain


def _inproj_kernel(x_ref, ln_ref, w_ref, qg_ref, kg_ref, hm_ref, u_ref, q_ref, k_ref, v_ref):
    hn = _rms(x_ref[0], ln_ref[...]).astype(BF16)
    proj = jnp.dot(hn, w_ref[...], preferred_element_type=F32)
    u_ref[...] = proj[:, :S5_WIDTH]
    hm = hm_ref[...]

    def head_norm(t, gain):
        ms = jnp.dot((t * t).astype(BF16), hm, preferred_element_type=F32)
        return t * lax.rsqrt(ms + EPS) * gain

    q = proj[:, S5_WIDTH:S5_WIDTH + SB_WIDTH]
    k = proj[:, S5_WIDTH + SB_WIDTH:S5_WIDTH + 2 * SB_WIDTH]
    q_ref[0] = (head_norm(q, qg_ref[...]) * (SB_HEAD_DIM ** -0.5 * LOG2_E)).astype(BF16)
    k_ref[0] = head_norm(k, kg_ref[...]).astype(BF16)
    v_ref[0] = proj[:, S5_WIDTH + 2 * SB_WIDTH:].astype(BF16)


def _inproj(x, ln, w_in, q_gain, k_gain, tl=512):
    bsz, seqlen, _ = x.shape
    heads = SB_WIDTH // SB_HEAD_DIM
    hm = jnp.kron(jnp.eye(heads, dtype=F32), jnp.full((SB_HEAD_DIM, SB_HEAD_DIM), 1.0 / SB_HEAD_DIM, F32)).astype(BF16)
    qg = jnp.tile(q_gain.astype(F32), heads)[None, :]
    kg = jnp.tile(k_gain.astype(F32), heads)[None, :]
    act = lambda: pl.BlockSpec((1, tl, SB_WIDTH), lambda b, i: (b, i, 0))
    return pl.pallas_call(
        _inproj_kernel,
        grid=(bsz, seqlen // tl),
        in_specs=[
            pl.BlockSpec((1, tl, D_MODEL), lambda b, i: (b, i, 0)),
            _const_spec((1, D_MODEL)),
            _const_spec((D_MODEL, IN_WIDTH)),
            _const_spec((1, SB_WIDTH)),
            _const_spec((1, SB_WIDTH)),
            _const_spec((SB_WIDTH, SB_WIDTH)),
        ],
        out_specs=[
            pl.BlockSpec((tl, S5_WIDTH), lambda b, i: (i, b)),
            act(), act(), act(),
        ],
        out_shape=[
            jax.ShapeDtypeStruct((seqlen, bsz * S5_WIDTH), F32),
            jax.ShapeDtypeStruct((bsz, seqlen, SB_WIDTH), BF16),
            jax.ShapeDtypeStruct((bsz, seqlen, SB_WIDTH), BF16),
            jax.ShapeDtypeStruct((bsz, seqlen, SB_WIDTH), BF16),
        ],
        compiler_params=pltpu.CompilerParams(
            dimension_semantics=("parallel", "parallel"), vmem_limit_bytes=VMEM_LIMIT),
        name="inproj",
    )(x, ln.astype(F32)[None, :], w_in.astype(BF16), qg, kg, hm)


S5_TILES_PER_PASS = 4


def _s5_kernel(u_ref, bw_ref, cw_ref, ar_ref, ai_ref, d_ref, wglu_ref, y_ref, st_ref, x_ref, *, steps):
    @pl.when(pl.program_id(0) == 0)
    def _():
        st_ref[...] = jnp.zeros_like(st_ref)

    u = u_ref[...]
    ub = u.astype(BF16)
    half_ch = S5_WIDTH // 2
    for kt in range(2):
        x_ref[:, kt * STATE_HALF:(kt + 1) * STATE_HALF] = jnp.dot(
            ub[:, kt * half_ch:(kt + 1) * half_ch], bw_ref[kt], preferred_element_type=F32)

    n_tiles = STATE_LANES // COL_TILE
    for t0 in range(0, n_tiles, S5_TILES_PER_PASS):
        tiles = range(t0, t0 + S5_TILES_PER_PASS)
        ar = [jnp.broadcast_to(ar_ref[:, c * LANES:(c + 1) * LANES], (SUBLANES, LANES)) for c in tiles]
        ai = [jnp.broadcast_to(ai_ref[:, c * LANES:(c + 1) * LANES], (SUBLANES, LANES)) for c in tiles]
        init = tuple(st_ref[:, c * COL_TILE + r * LANES:c * COL_TILE + (r + 1) * LANES]
                     for c in tiles for r in range(2))

        def body(t, carry, tiles=tiles, ar=ar, ai=ai):
            r0 = pl.multiple_of(t * SUBLANES, SUBLANES)
            out = []
            for n, c in enumerate(tiles):
                re_sl = slice(c * COL_TILE, c * COL_TILE + LANES)
                im_sl = slice(c * COL_TILE + LANES, (c + 1) * COL_TILE)
                xr, xi = carry[2 * n], carry[2 * n + 1]
                nxr = ar[n] * xr - ai[n] * xi + x_ref[pl.ds(r0, SUBLANES), re_sl]
                nxi = ar[n] * xi + ai[n] * xr + x_ref[pl.ds(r0, SUBLANES), im_sl]
                x_ref[pl.ds(r0, SUBLANES), re_sl] = nxr
                x_ref[pl.ds(r0, SUBLANES), im_sl] = nxi
                out += [nxr, nxi]
            return tuple(out)

        final = lax.fori_loop(0, steps, body, init, unroll=4)
        for n, c in enumerate(tiles):
            st_ref[:, c * COL_TILE:c * COL_TILE + LANES] = final[2 * n]
            st_ref[:, c * COL_TILE + LANES:(c + 1) * COL_TILE] = final[2 * n + 1]

    xb = x_ref[...].astype(BF16)
    y = jnp.concatenate(
        [jnp.dot(xb[:, kt * STATE_HALF:(kt + 1) * STATE_HALF], cw_ref[kt], preferred_element_type=F32)
         for kt in range(2)], axis=1)
    y = y + d_ref[...] * u
    y = jax.nn.gelu(y)
    g = jnp.dot(y.astype(BF16), wglu_ref[...], preferred_element_type=F32)
    y_ref[...] = (y * jax.nn.sigmoid(g)).astype(BF16)


def _s5_params(lam_re, lam_im, log_dt, b_re, b_im, c_re, c_im):
    lr = lam_re.astype(F32)
    li = lam_im.astype(F32)
    dt = jnp.exp(log_dt.astype(F32))[:, None]
    mag = jnp.exp(lr * dt)
    abar_r = mag * jnp.cos(li * dt)
    abar_i = mag * jnp.sin(li * dt)
    den = lr * lr + li * li
    nr = abar_r - 1.0
    ni = abar_i
    fr = (nr * lr + ni * li) / den
    fi = (ni * lr - nr * li) / den
    br = b_re.astype(F32)
    bi = b_im.astype(F32)
    bbar_r = fr[..., None] * br - fi[..., None] * bi
    bbar_i = fr[..., None] * bi + fi[..., None] * br
    g = jnp.arange(S5_GROUPS)
    pair = jax.nn.one_hot(g // 2, S5_GROUPS // 2, dtype=F32)
    par = jax.nn.one_hot(g % 2, 2, dtype=F32)
    bblk = jnp.stack([bbar_r.transpose(0, 2, 1), bbar_i.transpose(0, 2, 1)], axis=2)
    bfull = jnp.einsum('ghrp,gc,gq->ghcrqp', bblk, pair, par).reshape(S5_WIDTH, STATE_LANES)
    cblk = jnp.stack([c_re.astype(F32), -c_im.astype(F32)], axis=1)
    cfull = jnp.einsum('grhp,gc,gq->crqpgh', cblk, pair, par).reshape(STATE_LANES, S5_WIDTH)
    half_ch = S5_WIDTH // 2
    bw = jnp.stack([bfull[kt * half_ch:(kt + 1) * half_ch, kt * STATE_HALF:(kt + 1) * STATE_HALF]
                    for kt in range(2)]).astype(BF16)
    cw = jnp.stack([cfull[kt * STATE_HALF:(kt + 1) * STATE_HALF, kt * half_ch:(kt + 1) * half_ch]
                    for kt in range(2)]).astype(BF16)
    a_r = abar_r.reshape(1, S5_GROUPS * S5_STATE)
    a_i = abar_i.reshape(1, S5_GROUPS * S5_STATE)
    return bw, cw, a_r, a_i


def _s5(u_tb, bsz, lam_re, lam_im, log_dt, b_re, b_im, c_re, c_im, d, w_glu, steps=32):
    rows_total = u_tb.shape[0]
    assert bsz == SUBLANES
    rows = steps * SUBLANES
    bw, cw, a_r, a_i = _s5_params(lam_re, lam_im, log_dt, b_re, b_im, c_re, c_im)
    return pl.pallas_call(
        functools.partial(_s5_kernel, steps=steps),
        grid=(rows_total // rows,),
        in_specs=[
            pl.BlockSpec((rows, S5_WIDTH), lambda c: (c, 0)),
            _const_spec(bw.shape), _const_spec(cw.shape),
            _const_spec(a_r.shape), _const_spec(a_i.shape),
            _const_spec((1, S5_WIDTH)), _const_spec((S5_WIDTH, S5_WIDTH)),
        ],
        out_specs=pl.BlockSpec((rows, S5_WIDTH), lambda c: (c, 0)),
        out_shape=jax.ShapeDtypeStruct((rows_total, S5_WIDTH), BF16),
        scratch_shapes=[pltpu.VMEM((SUBLANES, STATE_LANES), F32), pltpu.VMEM((rows, STATE_LANES), F32)],
        compiler_params=pltpu.CompilerParams(
            dimension_semantics=("arbitrary",), vmem_limit_bytes=VMEM_LIMIT),
        name="s5",
    )(u_tb, bw, cw, a_r, a_i, d.astype(F32)[None, :], w_glu.astype(BF16))


def _attn_kernel(q_ref, k_ref, v_ref, o_ref, *, blk):
    i = pl.program_id(2)
    q = q_ref[0]
    lane = lax.broadcasted_iota(jnp.int32, q.shape, 1)
    zero = jnp.zeros_like(q)
    qh = (jnp.where(lane < SB_HEAD_DIM, q, zero), jnp.where(lane >= SB_HEAD_DIM, q, zero))
    row = lax.broadcasted_iota(jnp.int32, (blk, blk), 0)
    col = lax.broadcasted_iota(jnp.int32, (blk, blk), 1)
    later = (row > col).astype(BF16)
    causal = col < row

    def load_kv(j):
        k0 = pl.multiple_of(j * blk, blk)
        return k_ref[0, pl.ds(k0, blk), :], v_ref[0, pl.ds(k0, blk), :]

    def head_tile(q_h, kb, vb, run, diag):
        z = lax.dot_general(q_h, kb, (((1,), (1,)), ((), ())), preferred_element_type=F32)
        sp = jnp.log2(1.0 + jnp.exp2(-jnp.abs(z)))
        log_beta = jnp.minimum(z, 0.0) - sp
        log_1m = log_beta - z
        if diag:
            log_1m = jnp.where(causal, log_1m, 0.0)
        tail = jnp.dot(log_1m.astype(BF16), later, preferred_element_type=F32)
        expo = log_beta + tail
        if run is not None:
            expo = expo + run
        w = jnp.exp2(expo)
        if diag:
            w = jnp.where(causal, w, 0.0)
        pv = jnp.dot(w.astype(BF16), vb, preferred_element_type=F32)
        return pv, jnp.sum(log_1m, axis=1, keepdims=True)

    kb0, vb0 = load_kv(i)
    kb1, vb1 = load_kv(jnp.maximum(i - 1, 0))
    has_prev = (i > 0).astype(F32)
    carry = []
    for h in range(2):
        pv0, tot0 = head_tile(qh[h], kb0, vb0, None, True)
        pv1, tot1 = head_tile(qh[h], kb1, vb1, tot0, False)
        carry += [pv0 + has_prev * pv1, tot0 + has_prev * tot1]

    def live(carry):
        return jnp.max(jnp.maximum(carry[1], carry[3])) > ZERO_WEIGHT_LOG2

    def cond(state):
        jj, alive = state[0], state[1]
        return jnp.logical_and(jj <= i, alive)

    def body(state):
        jj, carry = state[0], state[2:]
        kb, vb = load_kv(i - jj)
        out = []
        for h in range(2):
            pv, tot = head_tile(qh[h], kb, vb, carry[2 * h + 1], False)
            out += [carry[2 * h] + pv, carry[2 * h + 1] + tot]
        return (jj + 1, live(out), *out)

    state = lax.while_loop(cond, body, (jnp.int32(2), live(carry), *carry))
    o_ref[0] = jnp.where(lane < SB_HEAD_DIM, state[2], state[4]).astype(BF16)


def _sbattn(q, k, v, blk=256):
    bsz, seqlen, _ = q.shape
    pairs = SB_WIDTH // LANES
    qspec = pl.BlockSpec((1, blk, LANES), lambda b, p, i: (b, i, p))
    kvspec = pl.BlockSpec((1, seqlen, LANES), lambda b, p, i: (b, 0, p))
    return pl.pallas_call(
        functools.partial(_attn_kernel, blk=blk),
        grid=(bsz, pairs, seqlen // blk),
        in_specs=[qspec, kvspec, kvspec],
        out_specs=qspec,
        out_shape=jax.ShapeDtypeStruct((bsz, seqlen, SB_WIDTH), BF16),
        compiler_params=pltpu.CompilerParams(
            dimension_semantics=("parallel", "parallel", "arbitrary"), vmem_limit_bytes=VMEM_LIMIT),
        name="sbattn",
    )(q, k, v)


FF_CHUNK = 1024


def _mlp_ple(h, p_ref, lnm_ref, wup_ref, wdn_ref, lnp_ref, wg_ref, wpu_ref, o_ref):
    hn = _rms(h, lnm_ref[...]).astype(BF16)
    acc = h
    for c in range(0, D_FF, FF_CHUNK):
        a = jnp.dot(hn, wup_ref[:, c:c + FF_CHUNK], preferred_element_type=F32)
        a = jnp.square(jnp.maximum(a, 0.0)).astype(BF16)
        acc = acc + jnp.dot(a, wdn_ref[c:c + FF_CHUNK, :], preferred_element_type=F32)
    gate = jax.nn.sigmoid(jnp.dot(_rms(acc, lnp_ref[...]).astype(BF16), wg_ref[...], preferred_element_type=F32))
    pe = jnp.dot(p_ref[0].astype(BF16), wpu_ref[...], preferred_element_type=F32)
    o_ref[0] = acc + pe * gate


def _layer_even_kernel(x_ref, s5_ref, sb_ref, wout_ref, p_ref, lnm_ref, wup_ref, wdn_ref, lnp_ref, wg_ref,
                       wpu_ref, o_ref):
    mixed = (jnp.dot(s5_ref[...], wout_ref[:S5_WIDTH, :], preferred_element_type=F32)
             + jnp.dot(sb_ref[0], wout_ref[S5_WIDTH:, :], preferred_element_type=F32))
    _mlp_ple(x_ref[0] + mixed, p_ref, lnm_ref, wup_ref, wdn_ref, lnp_ref, wg_ref, wpu_ref, o_ref)


def _layer_odd_kernel(h_ref, halo_ref, lno_ref, pw_ref, ps_ref, p_ref, lnm_ref, wup_ref, wdn_ref, lnp_ref,
                      wg_ref, wpu_ref, o_ref, *, tm):
    i = pl.program_id(1)
    h = h_ref[0]
    hn = _rms(h, lno_ref[...])
    halo = _rms(halo_ref[0], lno_ref[...]) * (i > 0).astype(F32)
    ext = jnp.concatenate([halo, hn], axis=0)
    t = lax.broadcasted_iota(jnp.int32, (tm, 1), 0) + i * tm
    outs = []
    for g, window in enumerate(POOL_WINDOWS):
        sl = slice(g * POOL_GROUP, (g + 1) * POOL_GROUP)
        s = ext[:, sl]
        span = 1
        while span < window:
            s = s + jnp.concatenate([jnp.zeros((span, POOL_GROUP), F32), s[:-span]], axis=0)
            span *= 2
        count = jnp.minimum(t + 1, window).astype(F32)
        y = s[POOL_HALO:] / count - hn[:, sl]
        outs.append(jnp.dot(y.astype(BF16), pw_ref[g], preferred_element_type=F32))
    mixed = jnp.concatenate(outs, axis=1) * ps_ref[...]
    _mlp_ple(h + mixed, p_ref, lnm_ref, wup_ref, wdn_ref, lnp_ref, wg_ref, wpu_ref, o_ref)


def _tail_specs(tm):
    return [
        pl.BlockSpec((1, tm, PLE_DIM), lambda b, i: (b, i, 0)),
        _const_spec((1, D_MODEL)), _const_spec((D_MODEL, D_FF)), _const_spec((D_FF, D_MODEL)),
        _const_spec((1, D_MODEL)), _const_spec((D_MODEL, D_MODEL)), _const_spec((PLE_DIM, D_MODEL)),
    ]


def _tail_args(p_i, ln_mlp, w_up, w_down, ln_ple, w_gate, w_ple_up):
    return (p_i, ln_mlp.astype(F32)[None, :], w_up.astype(BF16), w_down.astype(BF16),
            ln_ple.astype(F32)[None, :], w_gate.astype(BF16), w_ple_up.astype(BF16))


def _layer_even(x, s5_tb, sb, w_out, tail, tm=512):
    bsz, seqlen, _ = x.shape
    row = lambda w: pl.BlockSpec((1, tm, w), lambda b, i: (b, i, 0))
    return pl.pallas_call(
        _layer_even_kernel,
        grid=(bsz, seqlen // tm),
        in_specs=[row(D_MODEL),
                  pl.BlockSpec((tm, S5_WIDTH), lambda b, i: (i, b)),
                  row(SB_WIDTH),
                  _const_spec((D_MODEL, D_MODEL))] + _tail_specs(tm),
        out_specs=row(D_MODEL),
        out_shape=jax.ShapeDtypeStruct(x.shape, F32),
        compiler_params=pltpu.CompilerParams(
            dimension_semantics=("parallel", "parallel"), vmem_limit_bytes=VMEM_LIMIT),
        name="layer_even",
    )(x, s5_tb, sb, w_out.astype(BF16), *tail)


def _layer_odd(h, ln_odd, pool_w, pool_scale, tail, tm=512):
    bsz, seqlen, _ = h.shape
    row = lambda w: pl.BlockSpec((1, tm, w), lambda b, i: (b, i, 0))
    halo_blocks = tm // POOL_HALO
    return pl.pallas_call(
        functools.partial(_layer_odd_kernel, tm=tm),
        grid=(bsz, seqlen // tm),
        in_specs=[row(D_MODEL),
                  pl.BlockSpec((1, POOL_HALO, D_MODEL), lambda b, i: (b, jnp.maximum(i * halo_blocks - 1, 0), 0)),
                  _const_spec((1, D_MODEL)),
                  _const_spec((len(POOL_WINDOWS), POOL_GROUP, POOL_GROUP)),
                  _const_spec((1, D_MODEL))] + _tail_specs(tm),
        out_specs=row(D_MODEL),
        out_shape=jax.ShapeDtypeStruct(h.shape, F32),
        compiler_params=pltpu.CompilerParams(
            dimension_semantics=("parallel", "parallel"), vmem_limit_bytes=VMEM_LIMIT),
        name="layer_odd",
    )(h, h, ln_odd.astype(F32)[None, :], pool_w.astype(BF16), pool_scale.astype(F32)[None, :], *tail)


def kernel(x, p, ln_mix_even, w_in_even, s5_lambda_re, s5_lambda_im, s5_log_dt, s5_b_re, s5_b_im, s5_c_re,
           s5_c_im, s5_d, s5_w_glu, sb_q_gain, sb_k_gain, w_out_even, ln_mix_odd, pool_w, pool_scale, ln_mlp,
           w_mlp_up, w_mlp_down, ln_ple, w_ple_gate, w_ple_up):
    bsz, seqlen, _ = x.shape
    h = x
    for i in range(p.shape[0]):
        j = i // 2
        tail = _tail_args(p[i], ln_mlp[i], w_mlp_up[i], w_mlp_down[i], ln_ple[i], w_ple_gate[i], w_ple_up[i])
        if i % 2 == 0:
            u, q, k, v = _inproj(h, ln_mix_even[j], w_in_even[j], sb_q_gain[j], sb_k_gain[j])
            u_tb = u.reshape(seqlen * bsz, S5_WIDTH)
            s5_tb = _s5(u_tb, bsz, s5_lambda_re[j], s5_lambda_im[j], s5_log_dt[j], s5_b_re[j], s5_b_im[j],
                        s5_c_re[j], s5_c_im[j], s5_d[j], s5_w_glu[j])
            sb = _sbattn(q, k, v)
            h = _layer_even(h, s5_tb.reshape(seqlen, bsz * S5_WIDTH), sb, w_out_even[j], tail)
        else:
            h = _layer_odd(h, ln_mix_odd[j], pool_w[j], pool_scale[j], tail)
    return h
```

```python
import functools
import math

import jax
import jax.numpy as jnp
from jax import lax
from jax.experimental import pallas as pl
from jax.experimental.pallas import tpu as pltpu

F32 = jnp.float32
BF16 = jnp.bfloat16

D_MODEL = 1024
S5_WIDTH = 512
S5_GROUP = 16
S5_GROUPS = 32
S5_STATE = 64
SB_HEAD_DIM = 64
SB_WIDTH = 512
IN_WIDTH = S5_WIDTH + 3 * SB_WIDTH
POOL_WINDOWS = (2, 4, 8, 16)
POOL_GROUP = 256
POOL_HALO = 16
D_FF = 4 * D_MODEL
PLE_DIM = 256
EPS = 1e-6
LOG2_E = math.log2(math.e)
ZERO_WEIGHT_LOG2 = -150.0

LANES = 128
SUBLANES = 8
STATE_LANES = 2 * S5_GROUPS * S5_STATE
STATE_HALF = STATE_LANES // 2
COL_TILE = 2 * LANES

VMEM_LIMIT = 56 * 1024 * 1024


def _const_spec(shape):
    nd = len(shape)
    return pl.BlockSpec(shape, lambda *_: (0,) * nd, pipeline_mode=pl.Buffered(1))


def _rms(x, gain):
    ms = jnp.mean(x * x, axis=-1, keepdims=True)
    return x * lax.rsqrt(ms + EPS) * gain


def _inproj_kernel(x_ref, ln_ref, w_ref, qg_ref, kg_ref, hm_ref, u_ref, q_ref, k_ref, v_ref):
    hn = _rms(x_ref[0], ln_ref[...]).astype(BF16)
    proj = jnp.dot(hn, w_ref[...], preferred_element_type=F32)
    u_ref[0] = proj[:, :S5_WIDTH].astype(BF16)
    hm = hm_ref[...]

    def head_norm(t, gain):
        ms = jnp.dot((t * t).astype(BF16), hm, preferred_element_type=F32)
        return t * lax.rsqrt(ms + EPS) * gain

    q = proj[:, S5_WIDTH:S5_WIDTH + SB_WIDTH]
    k = proj[:, S5_WIDTH + SB_WIDTH:S5_WIDTH + 2 * SB_WIDTH]
    q_ref[0] = (head_norm(q, qg_ref[...]) * (SB_HEAD_DIM ** -0.5 * LOG2_E)).astype(BF16)
    k_ref[0] = head_norm(k, kg_ref[...]).astype(BF16)
    v_ref[0] = proj[:, S5_WIDTH + 2 * SB_WIDTH:].astype(BF16)


def _inproj(x, ln, w_in, q_gain, k_gain, tl=512):
    bsz, seqlen, _ = x.shape
    heads = SB_WIDTH // SB_HEAD_DIM
    hm = jnp.kron(jnp.eye(heads, dtype=F32), jnp.full((SB_HEAD_DIM, SB_HEAD_DIM), 1.0 / SB_HEAD_DIM, F32)).astype(BF16)
    qg = jnp.tile(q_gain.astype(F32), heads)[None, :]
    kg = jnp.tile(k_gain.astype(F32), heads)[None, :]
    act = lambda: pl.BlockSpec((1, tl, SB_WIDTH), lambda b, i: (b, i, 0))
    return pl.pallas_call(
        _inproj_kernel,
        grid=(bsz, seqlen // tl),
        in_specs=[
            pl.BlockSpec((1, tl, D_MODEL), lambda b, i: (b, i, 0)),
            _const_spec((1, D_MODEL)),
            _const_spec((D_MODEL, IN_WIDTH)),
            _const_spec((1, SB_WIDTH)),
            _const_spec((1, SB_WIDTH)),
            _const_spec((SB_WIDTH, SB_WIDTH)),
        ],
        out_specs=[act(), act(), act(), act()],
        out_shape=[
            jax.ShapeDtypeStruct((bsz, seqlen, S5_WIDTH), BF16),
            jax.ShapeDtypeStruct((bsz, seqlen, SB_WIDTH), BF16),
            jax.ShapeDtypeStruct((bsz, seqlen, SB_WIDTH), BF16),
            jax.ShapeDtypeStruct((bsz, seqlen, SB_WIDTH), BF16),
        ],
        compiler_params=pltpu.CompilerParams(
            dimension_semantics=("parallel", "parallel"), vmem_limit_bytes=VMEM_LIMIT),
        name="inproj",
    )(x, ln.astype(F32)[None, :], w_in.astype(BF16), qg, kg, hm)


S5_TILES_PER_PASS = 4


def _s5_kernel(u_ref, bw_ref, cw_ref, ar_ref, ai_ref, d_ref, wglu_ref, y_ref, st_ref, x_ref, *, steps):
    @pl.when(pl.program_id(0) == 0)
    def _():
        st_ref[...] = jnp.zeros_like(st_ref)

    rows = steps * SUBLANES
    r = lax.broadcasted_iota(jnp.int32, (rows, rows), 0)
    c = lax.broadcasted_iota(jnp.int32, (rows, rows), 1)
    to_tb = jnp.logical_and(r // SUBLANES == c % steps, r % SUBLANES == c // steps).astype(BF16)
    to_bt = jnp.logical_and(c // SUBLANES == r % steps, c % SUBLANES == r // steps).astype(BF16)
    u = jnp.dot(to_tb, u_ref[...].reshape(rows, S5_WIDTH), preferred_element_type=F32)
    ub = u.astype(BF16)
    half_ch = S5_WIDTH // 2
    for kt in range(2):
        x_ref[:, kt * STATE_HALF:(kt + 1) * STATE_HALF] = jnp.dot(
            ub[:, kt * half_ch:(kt + 1) * half_ch], bw_ref[kt], preferred_element_type=F32)

    n_tiles = STATE_LANES // COL_TILE
    for t0 in range(0, n_tiles, S5_TILES_PER_PASS):
        tiles = range(t0, t0 + S5_TILES_PER_PASS)
        ar = [jnp.broadcast_to(ar_ref[:, c * LANES:(c + 1) * LANES], (SUBLANES, LANES)) for c in tiles]
        ai = [jnp.broadcast_to(ai_ref[:, c * LANES:(c + 1) * LANES], (SUBLANES, LANES)) for c in tiles]
        init = tuple(st_ref[:, c * COL_TILE + r * LANES:c * COL_TILE + (r + 1) * LANES]
                     for c in tiles for r in range(2))

        def body(t, carry, tiles=tiles, ar=ar, ai=ai):
            r0 = pl.multiple_of(t * SUBLANES, SUBLANES)
            out = []
            for n, c in enumerate(tiles):
                re_sl = slice(c * COL_TILE, c * COL_TILE + LANES)
                im_sl = slice(c * COL_TILE + LANES, (c + 1) * COL_TILE)
                xr, xi = carry[2 * n], carry[2 * n + 1]
                nxr = ar[n] * xr - ai[n] * xi + x_ref[pl.ds(r0, SUBLANES), re_sl]
                nxi = ar[n] * xi + ai[n] * xr + x_ref[pl.ds(r0, SUBLANES), im_sl]
                x_ref[pl.ds(r0, SUBLANES), re_sl] = nxr
                x_ref[pl.ds(r0, SUBLANES), im_sl] = nxi
                out += [nxr, nxi]
            return tuple(out)

        final = lax.fori_loop(0, steps, body, init, unroll=4)
        for n, c in enumerate(tiles):
            st_ref[:, c * COL_TILE:c * COL_TILE + LANES] = final[2 * n]
            st_ref[:, c * COL_TILE + LANES:(c + 1) * COL_TILE] = final[2 * n + 1]

    xb = x_ref[...].astype(BF16)
    y = jnp.concatenate(
        [jnp.dot(xb[:, kt * STATE_HALF:(kt + 1) * STATE_HALF], cw_ref[kt], preferred_element_type=F32)
         for kt in range(2)], axis=1)
    y = y + d_ref[...] * u
    y = jax.nn.gelu(y)
    g = jnp.dot(y.astype(BF16), wglu_ref[...], preferred_element_type=F32)
    y = (y * jax.nn.sigmoid(g)).astype(BF16)
    y_ref[...] = jnp.dot(to_bt, y, preferred_element_type=F32).astype(BF16).reshape(SUBLANES, steps, S5_WIDTH)


def _s5_params(lam_re, lam_im, log_dt, b_re, b_im, c_re, c_im):
    lr = lam_re.astype(F32)
    li = lam_im.astype(F32)
    dt = jnp.exp(log_dt.astype(F32))[:, None]
    mag = jnp.exp(lr * dt)
    abar_r = mag * jnp.cos(li * dt)
    abar_i = mag * jnp.sin(li * dt)
    den = lr * lr + li * li
    nr = abar_r - 1.0
    ni = abar_i
    fr = (nr * lr + ni * li) / den
    fi = (ni * lr - nr * li) / den
    br = b_re.astype(F32)
    bi = b_im.astype(F32)
    bbar_r = fr[..., None] * br - fi[..., None] * bi
    bbar_i = fr[..., None] * bi + fi[..., None] * br
    half_ch = S5_WIDTH // 2
    rp = 2 * S5_STATE
    a_b = jnp.stack([bbar_r, bbar_i], axis=1).transpose(0, 3, 1, 2).reshape(2, half_ch, rp)
    a_c = (jnp.stack([c_re.astype(F32), -c_im.astype(F32)], axis=1).transpose(1, 3, 0, 2)
           .reshape(rp, 2, half_ch).transpose(1, 0, 2))
    lane = jnp.arange(STATE_HALF)
    lane_rp = ((lane // LANES) % 2) * S5_STATE + lane % S5_STATE
    lane_group = 2 * (lane // COL_TILE) + (lane // S5_STATE) % 2
    sel = (jnp.arange(rp)[:, None] == lane_rp[None, :]).astype(F32)
    mask = (jnp.arange(half_ch) // S5_GROUP)[:, None] == lane_group[None, :]
    exact = lax.Precision.HIGHEST
    bw = jnp.where(mask, jnp.einsum('krc,cl->krl', a_b, sel, precision=exact), 0.0).astype(BF16)
    cw = jnp.where(mask.T, jnp.einsum('lc,kcr->klr', sel.T, a_c, precision=exact), 0.0).astype(BF16)
    a_r = abar_r.reshape(1, S5_GROUPS * S5_STATE)
    a_i = abar_i.reshape(1, S5_GROUPS * S5_STATE)
    return bw, cw, a_r, a_i


def _s5(u, lam_re, lam_im, log_dt, b_re, b_im, c_re, c_im, d, w_glu, steps=32):
    bsz, seqlen, _ = u.shape
    assert bsz == SUBLANES
    rows = steps * SUBLANES
    blk = pl.BlockSpec((bsz, steps, S5_WIDTH), lambda c: (0, c, 0))
    bw, cw, a_r, a_i = _s5_params(lam_re, lam_im, log_dt, b_re, b_im, c_re, c_im)
    return pl.pallas_call(
        functools.partial(_s5_kernel, steps=steps),
        grid=(seqlen // steps,),
        in_specs=[
            blk,
            _const_spec(bw.shape), _const_spec(cw.shape),
            _const_spec(a_r.shape), _const_spec(a_i.shape),
            _const_spec((1, S5_WIDTH)), _const_spec((S5_WIDTH, S5_WIDTH)),
        ],
        out_specs=blk,
        out_shape=jax.ShapeDtypeStruct(u.shape, BF16),
        scratch_shapes=[pltpu.VMEM((SUBLANES, STATE_LANES), F32), pltpu.VMEM((rows, STATE_LANES), F32)],
        compiler_params=pltpu.CompilerParams(
            dimension_semantics=("arbitrary",), vmem_limit_bytes=VMEM_LIMIT),
        name="s5",
    )(u, bw, cw, a_r, a_i, d.astype(F32)[None, :], w_glu.astype(BF16))


def _attn_kernel(q_ref, k_ref, v_ref, o_ref, *, blk):
    i = pl.program_id(2)
    q = q_ref[0]
    lane = lax.broadcasted_iota(jnp.int32, q.shape, 1)
    zero = jnp.zeros_like(q)
    qh = (jnp.where(lane < SB_HEAD_DIM, q, zero), jnp.where(lane >= SB_HEAD_DIM, q, zero))
    row = lax.broadcasted_iota(jnp.int32, (blk, blk), 0)
    col = lax.broadcasted_iota(jnp.int32, (blk, blk), 1)
    later = (row > col).astype(BF16)
    causal = col < row

    def load_kv(j):
        k0 = pl.multiple_of(j * blk, blk)
        return k_ref[0, pl.ds(k0, blk), :], v_ref[0, pl.ds(k0, blk), :]

    def head_tile(q_h, kb, vb, run, diag):
        z = lax.dot_general(q_h, kb, (((1,), (1,)), ((), ())), preferred_element_type=F32)
        sp = jnp.log2(1.0 + jnp.exp2(-jnp.abs(z)))
        log_beta = jnp.minimum(z, 0.0) - sp
        log_1m = log_beta - z
        if diag:
            log_1m = jnp.where(causal, log_1m, 0.0)
        tail = jnp.dot(log_1m.astype(BF16), later, preferred_element_type=F32)
        expo = log_beta + tail
        if run is not None:
            expo = expo + run
        w = jnp.exp2(expo)
        if diag:
            w = jnp.where(causal, w, 0.0)
        pv = jnp.dot(w.astype(BF16), vb, preferred_element_type=F32)
        return pv, jnp.sum(log_1m, axis=1, keepdims=True)

    kb0, vb0 = load_kv(i)
    kb1, vb1 = load_kv(jnp.maximum(i - 1, 0))
    has_prev = (i > 0).astype(F32)
    carry = []
    for h in range(2):
        pv0, tot0 = head_tile(qh[h], kb0, vb0, None, True)
        pv1, tot1 = head_tile(qh[h], kb1, vb1, tot0, False)
        carry += [pv0 + has_prev * pv1, tot0 + has_prev * tot1]

    def live(carry):
        return jnp.max(jnp.maximum(carry[1], carry[3])) > ZERO_WEIGHT_LOG2

    def cond(state):
        jj, alive = state[0], state[1]
        return jnp.logical_and(jj <= i, alive)

    def body(state):
        jj, carry = state[0], state[2:]
        kb, vb = load_kv(i - jj)
        out = []
        for h in range(2):
            pv, tot = head_tile(qh[h], kb, vb, carry[2 * h + 1], False)
            out += [carry[2 * h] + pv, carry[2 * h + 1] + tot]
        return (jj + 1, live(out), *out)

    state = lax.while_loop(cond, body, (jnp.int32(2), live(carry), *carry))
    o_ref[0] = jnp.where(lane < SB_HEAD_DIM, state[2], state[4]).astype(BF16)


def _sbattn(q, k, v, blk=256):
    bsz, seqlen, _ = q.shape
    pairs = SB_WIDTH // LANES
    qspec = pl.BlockSpec((1, blk, LANES), lambda b, p, i: (b, i, p))
    kvspec = pl.BlockSpec((1, seqlen, LANES), lambda b, p, i: (b, 0, p))
    return pl.pallas_call(
        functools.partial(_attn_kernel, blk=blk),
        grid=(bsz, pairs, seqlen // blk),
        in_specs=[qspec, kvspec, kvspec],
        out_specs=qspec,
        out_shape=jax.ShapeDtypeStruct((bsz, seqlen, SB_WIDTH), BF16),
        compiler_params=pltpu.CompilerParams(
            dimension_semantics=("parallel", "parallel", "arbitrary"), vmem_limit_bytes=VMEM_LIMIT),
        name="sbattn",
    )(q, k, v)


FF_CHUNK = 1024


def _mlp_ple(h, p_ref, lnm_ref, wup_ref, wdn_ref, lnp_ref, wg_ref, wpu_ref, o_ref):
    hn = _rms(h, lnm_ref[...]).astype(BF16)
    acc = h
    for c in range(0, D_FF, FF_CHUNK):
        a = jnp.dot(hn, wup_ref[:, c:c + FF_CHUNK], preferred_element_type=F32)
        a = jnp.square(jnp.maximum(a, 0.0)).astype(BF16)
        acc = acc + jnp.dot(a, wdn_ref[c:c + FF_CHUNK, :], preferred_element_type=F32)
    gate = jax.nn.sigmoid(jnp.dot(_rms(acc, lnp_ref[...]).astype(BF16), wg_ref[...], preferred_element_type=F32))
    pe = jnp.dot(p_ref[0].astype(BF16), wpu_ref[...], preferred_element_type=F32)
    o_ref[0] = acc + pe * gate


def _layer_even_kernel(x_ref, s5_ref, sb_ref, wout_ref, p_ref, lnm_ref, wup_ref, wdn_ref, lnp_ref, wg_ref,
                       wpu_ref, o_ref):
    mixed = (jnp.dot(s5_ref[0], wout_ref[:S5_WIDTH, :], preferred_element_type=F32)
             + jnp.dot(sb_ref[0], wout_ref[S5_WIDTH:, :], preferred_element_type=F32))
    _mlp_ple(x_ref[0] + mixed, p_ref, lnm_ref, wup_ref, wdn_ref, lnp_ref, wg_ref, wpu_ref, o_ref)


def _layer_odd_kernel(h_ref, halo_ref, lno_ref, pw_ref, ps_ref, p_ref, lnm_ref, wup_ref, wdn_ref, lnp_ref,
                      wg_ref, wpu_ref, o_ref, *, tm):
    i = pl.program_id(1)
    h = h_ref[0]
    hn = _rms(h, lno_ref[...])
    halo = _rms(halo_ref[0], lno_ref[...]) * (i > 0).astype(F32)
    ext = jnp.concatenate([halo, hn], axis=0)
    t = lax.broadcasted_iota(jnp.int32, (tm, 1), 0) + i * tm
    outs = []
    for g, window in enumerate(POOL_WINDOWS):
        sl = slice(g * POOL_GROUP, (g + 1) * POOL_GROUP)
        s = ext[:, sl]
        span = 1
        while span < window:
            s = s + jnp.concatenate([jnp.zeros((span, POOL_GROUP), F32), s[:-span]], axis=0)
            span *= 2
        count = jnp.minimum(t + 1, window).astype(F32)
        y = s[POOL_HALO:] / count - hn[:, sl]
        outs.append(jnp.dot(y.astype(BF16), pw_ref[g], preferred_element_type=F32))
    mixed = jnp.concatenate(outs, axis=1) * ps_ref[...]
    _mlp_ple(h + mixed, p_ref, lnm_ref, wup_ref, wdn_ref, lnp_ref, wg_ref, wpu_ref, o_ref)


def _tail_specs(tm, layer):
    return [
        pl.BlockSpec((None, 1, tm, PLE_DIM), lambda b, i: (layer, b, i, 0)),
        _const_spec((1, D_MODEL)), _const_spec((D_MODEL, D_FF)), _const_spec((D_FF, D_MODEL)),
        _const_spec((1, D_MODEL)), _const_spec((D_MODEL, D_MODEL)), _const_spec((PLE_DIM, D_MODEL)),
    ]


def _tail_args(p, ln_mlp, w_up, w_down, ln_ple, w_gate, w_ple_up):
    return (p, ln_mlp.astype(F32)[None, :], w_up.astype(BF16), w_down.astype(BF16),
            ln_ple.astype(F32)[None, :], w_gate.astype(BF16), w_ple_up.astype(BF16))


def _layer_even(x, s5, sb, w_out, tail, layer, tm=512):
    bsz, seqlen, _ = x.shape
    row = lambda w: pl.BlockSpec((1, tm, w), lambda b, i: (b, i, 0))
    return pl.pallas_call(
        _layer_even_kernel,
        grid=(bsz, seqlen // tm),
        in_specs=[row(D_MODEL),
                  row(S5_WIDTH),
                  row(SB_WIDTH),
                  _const_spec((D_MODEL, D_MODEL))] + _tail_specs(tm, layer),
        out_specs=row(D_MODEL),
        out_shape=jax.ShapeDtypeStruct(x.shape, F32),
        compiler_params=pltpu.CompilerParams(
            dimension_semantics=("parallel", "parallel"), vmem_limit_bytes=VMEM_LIMIT),
        name="layer_even",
    )(x, s5, sb, w_out.astype(BF16), *tail)


def _layer_odd(h, ln_odd, pool_w, pool_scale, tail, layer, tm=512):
    bsz, seqlen, _ = h.shape
    row = lambda w: pl.BlockSpec((1, tm, w), lambda b, i: (b, i, 0))
    halo_blocks = tm // POOL_HALO
    return pl.pallas_call(
        functools.partial(_layer_odd_kernel, tm=tm),
        grid=(bsz, seqlen // tm),
        in_specs=[row(D_MODEL),
                  pl.BlockSpec((1, POOL_HALO, D_MODEL), lambda b, i: (b, jnp.maximum(i * halo_blocks - 1, 0), 0)),
                  _const_spec((1, D_MODEL)),
                  _const_spec((len(POOL_WINDOWS), POOL_GROUP, POOL_GROUP)),
                  _const_spec((1, D_MODEL))] + _tail_specs(tm, layer),
        out_specs=row(D_MODEL),
        out_shape=jax.ShapeDtypeStruct(h.shape, F32),
        compiler_params=pltpu.CompilerParams(
            dimension_semantics=("parallel", "parallel"), vmem_limit_bytes=VMEM_LIMIT),
        name="layer_odd",
    )(h, h, ln_odd.astype(F32)[None, :], pool_w.astype(BF16), pool_scale.astype(F32)[None, :], *tail)


def kernel(x, p, ln_mix_even, w_in_even, s5_lambda_re, s5_lambda_im, s5_log_dt, s5_b_re, s5_b_im, s5_c_re,
           s5_c_im, s5_d, s5_w_glu, sb_q_gain, sb_k_gain, w_out_even, ln_mix_odd, pool_w, pool_scale, ln_mlp,
           w_mlp_up, w_mlp_down, ln_ple, w_ple_gate, w_ple_up):
    h = x
    for i in range(p.shape[0]):
        j = i // 2
        tail = _tail_args(p, ln_mlp[i], w_mlp_up[i], w_mlp_down[i], ln_ple[i], w_ple_gate[i], w_ple_up[i])
        if i % 2 == 0:
            u, q, k, v = _inproj(h, ln_mix_even[j], w_in_even[j], sb_q_gain[j], sb_k_gain[j])
            s5 = _s5(u, s5_lambda_re[j], s5_lambda_im[j], s5_log_dt[j], s5_b_re[j], s5_b_im[j],
                     s5_c_re[j], s5_c_im[j], s5_d[j], s5_w_glu[j])
            sb = _sbattn(q, k, v)
            h = _layer_even(h, s5, sb, w_out_even[j], tail, i)
        else:
            h = _layer_odd(h, ln_mix_odd[j], pool_w[j], pool_scale[j], tail, i)
    return h
```

```python
import functools
import math

import jax
import jax.numpy as jnp
from jax import lax
from jax.experimental import pallas as pl
from jax.experimental.pallas import tpu as pltpu

F32 = jnp.float32
BF16 = jnp.bfloat16

D_MODEL = 1024
S5_WIDTH = 512
S5_GROUP = 16
S5_GROUPS = 32
S5_STATE = 64
SB_HEAD_DIM = 64
SB_WIDTH = 512
IN_WIDTH = S5_WIDTH + 3 * SB_WIDTH
POOL_WINDOWS = (2, 4, 8, 16)
POOL_GROUP = 256
POOL_HALO = 16
D_FF = 4 * D_MODEL
PLE_DIM = 256
EPS = 1e-6
LOG2_E = math.log2(math.e)
ZERO_WEIGHT_LOG2 = -150.0

LANES = 128
SUBLANES = 8
STATE_LANES = 2 * S5_GROUPS * S5_STATE
STATE_HALF = STATE_LANES // 2
COL_TILE = 2 * LANES

VMEM_LIMIT = 56 * 1024 * 1024


def _const_spec(shape):
    nd = len(shape)
    return pl.BlockSpec(shape, lambda *_: (0,) * nd, pipeline_mode=pl.Buffered(1))


def _rms(x, gain):
    ms = jnp.mean(x * x, axis=-1, keepdims=True)
    return x * lax.rsqrt(ms + EPS) * gain


def _inproj_kernel(x_ref, ln_ref, w_ref, qg_ref, kg_ref, hm_ref, u_ref, q_ref, k_ref, v_ref):
    hn = _rms(x_ref[0], ln_ref[...]).astype(BF16)
    proj = jnp.dot(hn, w_ref[...], preferred_element_type=F32)
    u_ref[0] = proj[:, :S5_WIDTH].astype(BF16)
    hm = hm_ref[...]

    def head_norm(t, gain):
        ms = jnp.dot((t * t).astype(BF16), hm, preferred_element_type=F32)
        return t * lax.rsqrt(ms + EPS) * gain

    q = proj[:, S5_WIDTH:S5_WIDTH + SB_WIDTH]
    k = proj[:, S5_WIDTH + SB_WIDTH:S5_WIDTH + 2 * SB_WIDTH]
    q_ref[0] = (head_norm(q, qg_ref[...]) * (SB_HEAD_DIM ** -0.5 * LOG2_E)).astype(BF16)
    k_ref[0] = head_norm(k, kg_ref[...]).astype(BF16)
    v_ref[0] = proj[:, S5_WIDTH + 2 * SB_WIDTH:].astype(BF16)


def _inproj(x, ln, w_in, q_gain, k_gain, tl=512):
    bsz, seqlen, _ = x.shape
    heads = SB_WIDTH // SB_HEAD_DIM
    hm = jnp.kron(jnp.eye(heads, dtype=F32), jnp.full((SB_HEAD_DIM, SB_HEAD_DIM), 1.0 / SB_HEAD_DIM, F32)).astype(BF16)
    qg = jnp.tile(q_gain.astype(F32), heads)[None, :]
    kg = jnp.tile(k_gain.astype(F32), heads)[None, :]
    act = lambda: pl.BlockSpec((1, tl, SB_WIDTH), lambda b, i: (b, i, 0))
    return pl.pallas_call(
        _inproj_kernel,
        grid=(bsz, seqlen // tl),
        in_specs=[
            pl.BlockSpec((1, tl, D_MODEL), lambda b, i: (b, i, 0)),
            _const_spec((1, D_MODEL)),
            _const_spec((D_MODEL, IN_WIDTH)),
            _const_spec((1, SB_WIDTH)),
            _const_spec((1, SB_WIDTH)),
            _const_spec((SB_WIDTH, SB_WIDTH)),
        ],
        out_specs=[act(), act(), act(), act()],
        out_shape=[
            jax.ShapeDtypeStruct((bsz, seqlen, S5_WIDTH), BF16),
            jax.ShapeDtypeStruct((bsz, seqlen, SB_WIDTH), BF16),
            jax.ShapeDtypeStruct((bsz, seqlen, SB_WIDTH), BF16),
            jax.ShapeDtypeStruct((bsz, seqlen, SB_WIDTH), BF16),
        ],
        compiler_params=pltpu.CompilerParams(
            dimension_semantics=("parallel", "parallel"), vmem_limit_bytes=VMEM_LIMIT),
        name="inproj",
    )(x, ln.astype(F32)[None, :], w_in.astype(BF16), qg, kg, hm)


S5_PERM_STEPS = 32


def _s5_kernel(u_ref, bw_ref, cw_ref, ar_ref, ai_ref, d_ref, wglu_ref, y_ref, st_ref, *, steps):
    @pl.when(pl.program_id(0) == 0)
    def _():
        st_ref[...] = jnp.zeros_like(st_ref)

    prow = S5_PERM_STEPS * SUBLANES
    r = lax.broadcasted_iota(jnp.int32, (prow, prow), 0)
    c = lax.broadcasted_iota(jnp.int32, (prow, prow), 1)
    to_tb = jnp.logical_and(r // SUBLANES == c % S5_PERM_STEPS, r % SUBLANES == c // S5_PERM_STEPS).astype(BF16)
    to_bt = jnp.logical_and(c // SUBLANES == r % S5_PERM_STEPS, c % SUBLANES == r // S5_PERM_STEPS).astype(BF16)
    u = jnp.concatenate(
        [jnp.dot(to_tb, u_ref[:, t0:t0 + S5_PERM_STEPS, :].reshape(prow, S5_WIDTH), preferred_element_type=F32)
         for t0 in range(0, steps, S5_PERM_STEPS)], axis=0)
    ub = u.astype(BF16)
    half_ch = S5_WIDTH // 2
    tiles_per_half = STATE_HALF // COL_TILE
    parts = []
    for kt in range(2):
        bu = jnp.dot(ub[:, kt * half_ch:(kt + 1) * half_ch], bw_ref[kt], preferred_element_type=F32)
        cols = []
        for cl in range(tiles_per_half):
            c = kt * tiles_per_half + cl
            ar = jnp.broadcast_to(ar_ref[:, c * LANES:(c + 1) * LANES], (SUBLANES, LANES))
            ai = jnp.broadcast_to(ai_ref[:, c * LANES:(c + 1) * LANES], (SUBLANES, LANES))
            xr = st_ref[:, c * COL_TILE:c * COL_TILE + LANES]
            xi = st_ref[:, c * COL_TILE + LANES:(c + 1) * COL_TILE]
            res_r, res_i = [], []
            for t in range(steps):
                rs = slice(t * SUBLANES, (t + 1) * SUBLANES)
                bur = bu[rs, cl * COL_TILE:cl * COL_TILE + LANES]
                bui = bu[rs, cl * COL_TILE + LANES:(cl + 1) * COL_TILE]
                xr, xi = ar * xr - ai * xi + bur, ar * xi + ai * xr + bui
                res_r.append(xr)
                res_i.append(xi)
            st_ref[:, c * COL_TILE:c * COL_TILE + LANES] = xr
            st_ref[:, c * COL_TILE + LANES:(c + 1) * COL_TILE] = xi
            cols += [jnp.concatenate(res_r, axis=0), jnp.concatenate(res_i, axis=0)]
        xh = jnp.concatenate(cols, axis=1).astype(BF16)
        parts.append(jnp.dot(xh, cw_ref[kt], preferred_element_type=F32))
    y = jnp.concatenate(parts, axis=1)
    y = y + d_ref[...] * u
    y = jax.nn.gelu(y)
    g = jnp.dot(y.astype(BF16), wglu_ref[...], preferred_element_type=F32)
    y = (y * jax.nn.sigmoid(g)).astype(BF16)
    for n, t0 in enumerate(range(0, steps, S5_PERM_STEPS)):
        y_bt = jnp.dot(to_bt, y[n * prow:(n + 1) * prow], preferred_element_type=F32).astype(BF16)
        y_ref[:, t0:t0 + S5_PERM_STEPS, :] = y_bt.reshape(SUBLANES, S5_PERM_STEPS, S5_WIDTH)


def _s5_params(lam_re, lam_im, log_dt, b_re, b_im, c_re, c_im):
    lr = lam_re.astype(F32)
    li = lam_im.astype(F32)
    dt = jnp.exp(log_dt.astype(F32))[:, None]
    mag = jnp.exp(lr * dt)
    abar_r = mag * jnp.cos(li * dt)
    abar_i = mag * jnp.sin(li * dt)
    den = lr * lr + li * li
    nr = abar_r - 1.0
    ni = abar_i
    fr = (nr * lr + ni * li) / den
    fi = (ni * lr - nr * li) / den
    br = b_re.astype(F32)
    bi = b_im.astype(F32)
    bbar_r = fr[..., None] * br - fi[..., None] * bi
    bbar_i = fr[..., None] * bi + fi[..., None] * br
    half_ch = S5_WIDTH // 2
    rp = 2 * S5_STATE
    a_b = jnp.stack([bbar_r, bbar_i], axis=1).transpose(0, 3, 1, 2).reshape(2, half_ch, rp)
    a_c = (jnp.stack([c_re.astype(F32), -c_im.astype(F32)], axis=1).transpose(1, 3, 0, 2)
           .reshape(rp, 2, half_ch).transpose(1, 0, 2))
    lane = jnp.arange(STATE_HALF)
    lane_rp = ((lane // LANES) % 2) * S5_STATE + lane % S5_STATE
    lane_group = 2 * (lane // COL_TILE) + (lane // S5_STATE) % 2
    sel = (jnp.arange(rp)[:, None] == lane_rp[None, :]).astype(F32)
    mask = (jnp.arange(half_ch) // S5_GROUP)[:, None] == lane_group[None, :]
    exact = lax.Precision.HIGHEST
    bw = jnp.where(mask, jnp.einsum('krc,cl->krl', a_b, sel, precision=exact), 0.0).astype(BF16)
    cw = jnp.where(mask.T, jnp.einsum('lc,kcr->klr', sel.T, a_c, precision=exact), 0.0).astype(BF16)
    a_r = abar_r.reshape(1, S5_GROUPS * S5_STATE)
    a_i = abar_i.reshape(1, S5_GROUPS * S5_STATE)
    return bw, cw, a_r, a_i


def _s5(u, lam_re, lam_im, log_dt, b_re, b_im, c_re, c_im, d, w_glu, steps=64):
    bsz, seqlen, _ = u.shape
    assert bsz == SUBLANES
    blk = pl.BlockSpec((bsz, steps, S5_WIDTH), lambda c: (0, c, 0))
    bw, cw, a_r, a_i = _s5_params(lam_re, lam_im, log_dt, b_re, b_im, c_re, c_im)
    return pl.pallas_call(
        functools.partial(_s5_kernel, steps=steps),
        grid=(seqlen // steps,),
        in_specs=[
            blk,
            _const_spec(bw.shape), _const_spec(cw.shape),
            _const_spec(a_r.shape), _const_spec(a_i.shape),
            _const_spec((1, S5_WIDTH)), _const_spec((S5_WIDTH, S5_WIDTH)),
        ],
        out_specs=blk,
        out_shape=jax.ShapeDtypeStruct(u.shape, BF16),
        scratch_shapes=[pltpu.VMEM((SUBLANES, STATE_LANES), F32)],
        compiler_params=pltpu.CompilerParams(
            dimension_semantics=("arbitrary",), vmem_limit_bytes=VMEM_LIMIT),
        name="s5",
    )(u, bw, cw, a_r, a_i, d.astype(F32)[None, :], w_glu.astype(BF16))


def _attn_kernel(q_ref, k_ref, v_ref, o_ref, *, blk):
    i = pl.program_id(2)
    q = q_ref[0]
    lane = lax.broadcasted_iota(jnp.int32, q.shape, 1)
    zero = jnp.zeros_like(q)
    qh = (jnp.where(lane < SB_HEAD_DIM, q, zero), jnp.where(lane >= SB_HEAD_DIM, q, zero))
    row = lax.broadcasted_iota(jnp.int32, (blk, blk), 0)
    col = lax.broadcasted_iota(jnp.int32, (blk, blk), 1)
    later = (row > col).astype(BF16)
    causal = col < row

    def load_kv(j):
        k0 = pl.multiple_of(j * blk, blk)
        return k_ref[0, pl.ds(k0, blk), :], v_ref[0, pl.ds(k0, blk), :]

    def head_tile(q_h, kb, vb, run, diag):
        z = lax.dot_general(q_h, kb, (((1,), (1,)), ((), ())), preferred_element_type=F32)
        sp = jnp.log2(1.0 + jnp.exp2(-jnp.abs(z)))
        log_beta = jnp.minimum(z, 0.0) - sp
        log_1m = log_beta - z
        if diag:
            log_1m = jnp.where(causal, log_1m, 0.0)
        tail = jnp.dot(log_1m.astype(BF16), later, preferred_element_type=F32)
        expo = log_beta + tail
        if run is not None:
            expo = expo + run
        w = jnp.exp2(expo)
        if diag:
            w = jnp.where(causal, w, 0.0)
        pv = jnp.dot(w.astype(BF16), vb, preferred_element_type=F32)
        return pv, jnp.sum(log_1m, axis=1, keepdims=True)

    kb0, vb0 = load_kv(i)
    kb1, vb1 = load_kv(jnp.maximum(i - 1, 0))
    has_prev = (i > 0).astype(F32)
    carry = []
    for h in range(2):
        pv0, tot0 = head_tile(qh[h], kb0, vb0, None, True)
        pv1, tot1 = head_tile(qh[h], kb1, vb1, tot0, False)
        carry += [pv0 + has_prev * pv1, tot0 + has_prev * tot1]

    def live(carry):
        return jnp.max(jnp.maximum(carry[1], carry[3])) > ZERO_WEIGHT_LOG2

    def cond(state):
        jj, alive = state[0], state[1]
        return jnp.logical_and(jj <= i, alive)

    def body(state):
        jj, carry = state[0], state[2:]
        kb, vb = load_kv(i - jj)
        out = []
        for h in range(2):
            pv, tot = head_tile(qh[h], kb, vb, carry[2 * h + 1], False)
            out += [carry[2 * h] + pv, carry[2 * h + 1] + tot]
        return (jj + 1, live(out), *out)

    state = lax.while_loop(cond, body, (jnp.int32(2), live(carry), *carry))
    o_ref[0] = jnp.where(lane < SB_HEAD_DIM, state[2], state[4]).astype(BF16)


def _sbattn(q, k, v, blk=256):
    bsz, seqlen, _ = q.shape
    pairs = SB_WIDTH // LANES
    qspec = pl.BlockSpec((1, blk, LANES), lambda b, p, i: (b, i, p))
    kvspec = pl.BlockSpec((1, seqlen, LANES), lambda b, p, i: (b, 0, p))
    return pl.pallas_call(
        functools.partial(_attn_kernel, blk=blk),
        grid=(bsz, pairs, seqlen // blk),
        in_specs=[qspec, kvspec, kvspec],
        out_specs=qspec,
        out_shape=jax.ShapeDtypeStruct((bsz, seqlen, SB_WIDTH), BF16),
        compiler_params=pltpu.CompilerParams(
            dimension_semantics=("parallel", "parallel", "arbitrary"), vmem_limit_bytes=VMEM_LIMIT),
        name="sbattn",
    )(q, k, v)


FF_CHUNK = 1024


def _mlp_ple(h, p_ref, lnm_ref, wup_ref, wdn_ref, lnp_ref, wg_ref, wpu_ref, o_ref):
    hn = _rms(h, lnm_ref[...]).astype(BF16)
    acc = h
    for c in range(0, D_FF, FF_CHUNK):
        a = jnp.dot(hn, wup_ref[:, c:c + FF_CHUNK], preferred_element_type=F32)
        a = jnp.square(jnp.maximum(a, 0.0)).astype(BF16)
        acc = acc + jnp.dot(a, wdn_ref[c:c + FF_CHUNK, :], preferred_element_type=F32)
    gate = jax.nn.sigmoid(jnp.dot(_rms(acc, lnp_ref[...]).astype(BF16), wg_ref[...], preferred_element_type=F32))
    pe = jnp.dot(p_ref[0].astype(BF16), wpu_ref[...], preferred_element_type=F32)
    o_ref[0] = acc + pe * gate


def _layer_even_kernel(x_ref, s5_ref, sb_ref, wout_ref, p_ref, lnm_ref, wup_ref, wdn_ref, lnp_ref, wg_ref,
                       wpu_ref, o_ref):
    mixed = (jnp.dot(s5_ref[0], wout_ref[:S5_WIDTH, :], preferred_element_type=F32)
             + jnp.dot(sb_ref[0], wout_ref[S5_WIDTH:, :], preferred_element_type=F32))
    _mlp_ple(x_ref[0] + mixed, p_ref, lnm_ref, wup_ref, wdn_ref, lnp_ref, wg_ref, wpu_ref, o_ref)


def _layer_odd_kernel(h_ref, halo_ref, lno_ref, pw_ref, ps_ref, p_ref, lnm_ref, wup_ref, wdn_ref, lnp_ref,
                      wg_ref, wpu_ref, o_ref, *, tm):
    i = pl.program_id(1)
    h = h_ref[0]
    hn = _rms(h, lno_ref[...])
    halo = _rms(halo_ref[0], lno_ref[...]) * (i > 0).astype(F32)
    ext = jnp.concatenate([halo, hn], axis=0)
    t = lax.broadcasted_iota(jnp.int32, (tm, 1), 0) + i * tm
    outs = []
    for g, window in enumerate(POOL_WINDOWS):
        sl = slice(g * POOL_GROUP, (g + 1) * POOL_GROUP)
        s = ext[:, sl]
        span = 1
        while span < window:
            s = s + jnp.concatenate([jnp.zeros((span, POOL_GROUP), F32), s[:-span]], axis=0)
            span *= 2
        count = jnp.minimum(t + 1, window).astype(F32)
        y = s[POOL_HALO:] / count - hn[:, sl]
        outs.append(jnp.dot(y.astype(BF16), pw_ref[g], preferred_element_type=F32))
    mixed = jnp.concatenate(outs, axis=1) * ps_ref[...]
    _mlp_ple(h + mixed, p_ref, lnm_ref, wup_ref, wdn_ref, lnp_ref, wg_ref, wpu_ref, o_ref)


def _tail_specs(tm, layer):
    return [
        pl.BlockSpec((None, 1, tm, PLE_DIM), lambda b, i: (layer, b, i, 0)),
        _const_spec((1, D_MODEL)), _const_spec((D_MODEL, D_FF)), _const_spec((D_FF, D_MODEL)),
        _const_spec((1, D_MODEL)), _const_spec((D_MODEL, D_MODEL)), _const_spec((PLE_DIM, D_MODEL)),
    ]


def _tail_args(p, ln_mlp, w_up, w_down, ln_ple, w_gate, w_ple_up):
    return (p, ln_mlp.astype(F32)[None, :], w_up.astype(BF16), w_down.astype(BF16),
            ln_ple.astype(F32)[None, :], w_gate.astype(BF16), w_ple_up.astype(BF16))


def _layer_even(x, s5, sb, w_out, tail, layer, tm=512):
    bsz, seqlen, _ = x.shape
    row = lambda w: pl.BlockSpec((1, tm, w), lambda b, i: (b, i, 0))
    return pl.pallas_call(
        _layer_even_kernel,
        grid=(bsz, seqlen // tm),
        in_specs=[row(D_MODEL),
                  row(S5_WIDTH),
                  row(SB_WIDTH),
                  _const_spec((D_MODEL, D_MODEL))] + _tail_specs(tm, layer),
        out_specs=row(D_MODEL),
        out_shape=jax.ShapeDtypeStruct(x.shape, F32),
        compiler_params=pltpu.CompilerParams(
            dimension_semantics=("parallel", "parallel"), vmem_limit_bytes=VMEM_LIMIT),
        name="layer_even",
    )(x, s5, sb, w_out.astype(BF16), *tail)


def _layer_odd(h, ln_odd, pool_w, pool_scale, tail, layer, tm=512):
    bsz, seqlen, _ = h.shape
    row = lambda w: pl.BlockSpec((1, tm, w), lambda b, i: (b, i, 0))
    halo_blocks = tm // POOL_HALO
    return pl.pallas_call(
        functools.partial(_layer_odd_kernel, tm=tm),
        grid=(bsz, seqlen // tm),
        in_specs=[row(D_MODEL),
                  pl.BlockSpec((1, POOL_HALO, D_MODEL), lambda b, i: (b, jnp.maximum(i * halo_blocks - 1, 0), 0)),
                  _const_spec((1, D_MODEL)),
                  _const_spec((len(POOL_WINDOWS), POOL_GROUP, POOL_GROUP)),
                  _const_spec((1, D_MODEL))] + _tail_specs(tm, layer),
        out_specs=row(D_MODEL),
        out_shape=jax.ShapeDtypeStruct(h.shape, F32),
        compiler_params=pltpu.CompilerParams(
            dimension_semantics=("parallel", "parallel"), vmem_limit_bytes=VMEM_LIMIT),
        name="layer_odd",
    )(h, h, ln_odd.astype(F32)[None, :], pool_w.astype(BF16), pool_scale.astype(F32)[None, :], *tail)


def kernel(x, p, ln_mix_even, w_in_even, s5_lambda_re, s5_lambda_im, s5_log_dt, s5_b_re, s5_b_im, s5_c_re,
           s5_c_im, s5_d, s5_w_glu, sb_q_gain, sb_k_gain, w_out_even, ln_mix_odd, pool_w, pool_scale, ln_mlp,
           w_mlp_up, w_mlp_down, ln_ple, w_ple_gate, w_ple_up):
    h = x
    for i in range(p.shape[0]):
        j = i // 2
        tail = _tail_args(p, ln_mlp[i], w_mlp_up[i], w_mlp_down[i], ln_ple[i], w_ple_gate[i], w_ple_up[i])
        if i % 2 == 0:
            u, q, k, v = _inproj(h, ln_mix_even[j], w_in_even[j], sb_q_gain[j], sb_k_gain[j])
            s5 = _s5(u, s5_lambda_re[j], s5_lambda_im[j], s5_log_dt[j], s5_b_re[j], s5_b_im[j],
                     s5_c_re[j], s5_c_im[j], s5_d[j], s5_w_glu[j])
            sb = _sbattn(q, k, v)
            h = _layer_even(h, s5, sb, w_out_even[j], tail, i)
        else:
            h = _layer_odd(h, ln_mix_odd[j], pool_w[j], pool_scale[j], tail, i)
    return h
```

```python
import functools
import math

import jax
import jax.numpy as jnp
from jax import lax
from jax.experimental import pallas as pl
from jax.experimental.pallas import tpu as pltpu

F32 = jnp.float32
BF16 = jnp.bfloat16

D_MODEL = 1024
S5_WIDTH = 512
S5_GROUP = 16
S5_GROUPS = 32
S5_STATE = 64
SB_HEAD_DIM = 64
SB_WIDTH = 512
IN_WIDTH = S5_WIDTH + 3 * SB_WIDTH
POOL_WINDOWS = (2, 4, 8, 16)
POOL_GROUP = 256
POOL_HALO = 16
D_FF = 4 * D_MODEL
PLE_DIM = 256
EPS = 1e-6
LOG2_E = math.log2(math.e)
ZERO_WEIGHT_LOG2 = -150.0

LANES = 128
SUBLANES = 8
STATE_LANES = 2 * S5_GROUPS * S5_STATE
STATE_HALF = STATE_LANES // 2
COL_TILE = 2 * LANES

VMEM_LIMIT = 56 * 1024 * 1024


def _const_spec(shape):
    nd = len(shape)
    return pl.BlockSpec(shape, lambda *_: (0,) * nd, pipeline_mode=pl.Buffered(1))


def _rms(x, gain):
    ms = jnp.mean(x * x, axis=-1, keepdims=True)
    return x * lax.rsqrt(ms + EPS) * gain


def _inproj_kernel(x_ref, ln_ref, w_ref, qg_ref, kg_ref, hm_ref, u_ref, q_ref, k_ref, v_ref):
    hn = _rms(x_ref[0], ln_ref[...]).astype(BF16)
    proj = jnp.dot(hn, w_ref[...], preferred_element_type=F32)
    u_ref[0] = proj[:, :S5_WIDTH].astype(BF16)
    hm = hm_ref[...]

    def head_norm(t, gain):
        ms = jnp.dot((t * t).astype(BF16), hm, preferred_element_type=F32)
        return t * lax.rsqrt(ms + EPS) * gain

    q = proj[:, S5_WIDTH:S5_WIDTH + SB_WIDTH]
    k = proj[:, S5_WIDTH + SB_WIDTH:S5_WIDTH + 2 * SB_WIDTH]
    q_ref[0] = (head_norm(q, qg_ref[...]) * (SB_HEAD_DIM ** -0.5 * LOG2_E)).astype(BF16)
    k_ref[0] = head_norm(k, kg_ref[...]).astype(BF16)
    v_ref[0] = proj[:, S5_WIDTH + 2 * SB_WIDTH:].astype(BF16)


def _inproj(x, ln, w_in, q_gain, k_gain, tl=512):
    bsz, seqlen, _ = x.shape
    heads = SB_WIDTH // SB_HEAD_DIM
    hm = jnp.kron(jnp.eye(heads, dtype=F32), jnp.full((SB_HEAD_DIM, SB_HEAD_DIM), 1.0 / SB_HEAD_DIM, F32)).astype(BF16)
    qg = jnp.tile(q_gain.astype(F32), heads)[None, :]
    kg = jnp.tile(k_gain.astype(F32), heads)[None, :]
    act = lambda: pl.BlockSpec((1, tl, SB_WIDTH), lambda b, i: (b, i, 0))
    return pl.pallas_call(
        _inproj_kernel,
        grid=(bsz, seqlen // tl),
        in_specs=[
            pl.BlockSpec((1, tl, D_MODEL), lambda b, i: (b, i, 0)),
            _const_spec((1, D_MODEL)),
            _const_spec((D_MODEL, IN_WIDTH)),
            _const_spec((1, SB_WIDTH)),
            _const_spec((1, SB_WIDTH)),
            _const_spec((SB_WIDTH, SB_WIDTH)),
        ],
        out_specs=[act(), act(), act(), act()],
        out_shape=[
            jax.ShapeDtypeStruct((bsz, seqlen, S5_WIDTH), BF16),
            jax.ShapeDtypeStruct((bsz, seqlen, SB_WIDTH), BF16),
            jax.ShapeDtypeStruct((bsz, seqlen, SB_WIDTH), BF16),
            jax.ShapeDtypeStruct((bsz, seqlen, SB_WIDTH), BF16),
        ],
        compiler_params=pltpu.CompilerParams(
            dimension_semantics=("parallel", "parallel"), vmem_limit_bytes=VMEM_LIMIT),
        name="inproj",
    )(x, ln.astype(F32)[None, :], w_in.astype(BF16), qg, kg, hm)


S5_PERM_STEPS = 32


def _s5_kernel(u_ref, bw_ref, cw_ref, ar_ref, ai_ref, d_ref, wglu_ref, y_ref, st_ref, *, steps):
    @pl.when(pl.program_id(0) == 0)
    def _():
        st_ref[...] = jnp.zeros_like(st_ref)

    prow = S5_PERM_STEPS * SUBLANES
    r = lax.broadcasted_iota(jnp.int32, (prow, prow), 0)
    c = lax.broadcasted_iota(jnp.int32, (prow, prow), 1)
    to_tb = jnp.logical_and(r // SUBLANES == c % S5_PERM_STEPS, r % SUBLANES == c // S5_PERM_STEPS).astype(BF16)
    to_bt = jnp.logical_and(c // SUBLANES == r % S5_PERM_STEPS, c % SUBLANES == r // S5_PERM_STEPS).astype(BF16)
    u = jnp.concatenate(
        [jnp.dot(to_tb, u_ref[:, t0:t0 + S5_PERM_STEPS, :].reshape(prow, S5_WIDTH), preferred_element_type=F32)
         for t0 in range(0, steps, S5_PERM_STEPS)], axis=0)
    ub = u.astype(BF16)
    half_ch = S5_WIDTH // 2
    tiles_per_half = STATE_HALF // COL_TILE
    parts = []
    for kt in range(2):
        bu = jnp.dot(ub[:, kt * half_ch:(kt + 1) * half_ch], bw_ref[kt], preferred_element_type=F32)
        cols = []
        for cl in range(tiles_per_half):
            c = kt * tiles_per_half + cl
            ar = jnp.broadcast_to(ar_ref[:, c * LANES:(c + 1) * LANES], (SUBLANES, LANES))
            ai = jnp.broadcast_to(ai_ref[:, c * LANES:(c + 1) * LANES], (SUBLANES, LANES))
            xr = st_ref[:, c * COL_TILE:c * COL_TILE + LANES]
            xi = st_ref[:, c * COL_TILE + LANES:(c + 1) * COL_TILE]
            res_r, res_i = [], []
            for t in range(steps):
                rs = slice(t * SUBLANES, (t + 1) * SUBLANES)
                bur = bu[rs, cl * COL_TILE:cl * COL_TILE + LANES]
                bui = bu[rs, cl * COL_TILE + LANES:(cl + 1) * COL_TILE]
                xr, xi = ar * xr - ai * xi + bur, ar * xi + ai * xr + bui
                res_r.append(xr)
                res_i.append(xi)
            st_ref[:, c * COL_TILE:c * COL_TILE + LANES] = xr
            st_ref[:, c * COL_TILE + LANES:(c + 1) * COL_TILE] = xi
            cols += [jnp.concatenate(res_r, axis=0), jnp.concatenate(res_i, axis=0)]
        xh = jnp.concatenate(cols, axis=1).astype(BF16)
        parts.append(jnp.dot(xh, cw_ref[kt], preferred_element_type=F32))
    y = jnp.concatenate(parts, axis=1)
    y = y + d_ref[...] * u
    y = jax.nn.gelu(y)
    g = jnp.dot(y.astype(BF16), wglu_ref[...], preferred_element_type=F32)
    y = (y * jax.nn.sigmoid(g)).astype(BF16)
    for n, t0 in enumerate(range(0, steps, S5_PERM_STEPS)):
        y_bt = jnp.dot(to_bt, y[n * prow:(n + 1) * prow], preferred_element_type=F32).astype(BF16)
        y_ref[:, t0:t0 + S5_PERM_STEPS, :] = y_bt.reshape(SUBLANES, S5_PERM_STEPS, S5_WIDTH)


def _s5_params(lam_re, lam_im, log_dt, b_re, b_im, c_re, c_im):
    lr = lam_re.astype(F32)
    li = lam_im.astype(F32)
    dt = jnp.exp(log_dt.astype(F32))[:, None]
    mag = jnp.exp(lr * dt)
    abar_r = mag * jnp.cos(li * dt)
    abar_i = mag * jnp.sin(li * dt)
    den = lr * lr + li * li
    nr = abar_r - 1.0
    ni = abar_i
    fr = (nr * lr + ni * li) / den
    fi = (ni * lr - nr * li) / den
    br = b_re.astype(F32)
    bi = b_im.astype(F32)
    bbar_r = fr[..., None] * br - fi[..., None] * bi
    bbar_i = fr[..., None] * bi + fi[..., None] * br
    half_ch = S5_WIDTH // 2
    rp = 2 * S5_STATE
    a_b = jnp.stack([bbar_r, bbar_i], axis=1).transpose(0, 3, 1, 2).reshape(2, half_ch, rp)
    a_c = (jnp.stack([c_re.astype(F32), -c_im.astype(F32)], axis=1).transpose(1, 3, 0, 2)
           .reshape(rp, 2, half_ch).transpose(1, 0, 2))
    lane = jnp.arange(STATE_HALF)
    lane_rp = ((lane // LANES) % 2) * S5_STATE + lane % S5_STATE
    lane_group = 2 * (lane // COL_TILE) + (lane // S5_STATE) % 2
    sel = (jnp.arange(rp)[:, None] == lane_rp[None, :]).astype(F32)
    mask = (jnp.arange(half_ch) // S5_GROUP)[:, None] == lane_group[None, :]
    exact = lax.Precision.HIGHEST
    bw = jnp.where(mask, jnp.einsum('krc,cl->krl', a_b, sel, precision=exact), 0.0).astype(BF16)
    cw = jnp.where(mask.T, jnp.einsum('lc,kcr->klr', sel.T, a_c, precision=exact), 0.0).astype(BF16)
    a_r = abar_r.reshape(1, S5_GROUPS * S5_STATE)
    a_i = abar_i.reshape(1, S5_GROUPS * S5_STATE)
    return bw, cw, a_r, a_i


def _s5(u, lam_re, lam_im, log_dt, b_re, b_im, c_re, c_im, d, w_glu, steps=64):
    bsz, seqlen, _ = u.shape
    assert bsz == SUBLANES
    blk = pl.BlockSpec((bsz, steps, S5_WIDTH), lambda c: (0, c, 0))
    bw, cw, a_r, a_i = _s5_params(lam_re, lam_im, log_dt, b_re, b_im, c_re, c_im)
    return pl.pallas_call(
        functools.partial(_s5_kernel, steps=steps),
        grid=(seqlen // steps,),
        in_specs=[
            blk,
            _const_spec(bw.shape), _const_spec(cw.shape),
            _const_spec(a_r.shape), _const_spec(a_i.shape),
            _const_spec((1, S5_WIDTH)), _const_spec((S5_WIDTH, S5_WIDTH)),
        ],
        out_specs=blk,
        out_shape=jax.ShapeDtypeStruct(u.shape, BF16),
        scratch_shapes=[pltpu.VMEM((SUBLANES, STATE_LANES), F32)],
        compiler_params=pltpu.CompilerParams(
            dimension_semantics=("arbitrary",), vmem_limit_bytes=VMEM_LIMIT),
        name="s5",
    )(u, bw, cw, a_r, a_i, d.astype(F32)[None, :], w_glu.astype(BF16))


ATTN_GROUP_LANES = 256


def _attn_kernel(q_ref, k_ref, v_ref, o_ref, *, blk):
    i = pl.program_id(1)
    n_groups = q_ref.shape[-1] // ATTN_GROUP_LANES
    heads = ATTN_GROUP_LANES // SB_HEAD_DIM
    head_of_lane = lax.broadcasted_iota(jnp.int32, (blk, ATTN_GROUP_LANES), 1) // SB_HEAD_DIM
    row = lax.broadcasted_iota(jnp.int32, (blk, blk), 0)
    col = lax.broadcasted_iota(jnp.int32, (blk, blk), 1)
    later = (row > col).astype(BF16)
    causal = col < row

    def lanes(g):
        return slice(g * ATTN_GROUP_LANES, (g + 1) * ATTN_GROUP_LANES)

    def load_kv(j, g):
        k0 = pl.multiple_of(j * blk, blk)
        return k_ref[0, pl.ds(k0, blk), lanes(g)], v_ref[0, pl.ds(k0, blk), lanes(g)]

    qh = []
    for g in range(n_groups):
        qg = q_ref[0, :, lanes(g)]
        qh.append([jnp.where(head_of_lane == h, qg, jnp.zeros_like(qg)) for h in range(heads)])

    def head_tile(q_h, kb, vb, run, diag):
        z = lax.dot_general(q_h, kb, (((1,), (1,)), ((), ())), preferred_element_type=F32)
        sp = jnp.log2(1.0 + jnp.exp2(-jnp.abs(z)))
        log_beta = jnp.minimum(z, 0.0) - sp
        log_1m = log_beta - z
        if diag:
            log_1m = jnp.where(causal, log_1m, 0.0)
        tail = jnp.dot(log_1m.astype(BF16), later, preferred_element_type=F32)
        expo = log_beta + tail
        if run is not None:
            expo = expo + run
        w = jnp.exp2(expo)
        if diag:
            w = jnp.where(causal, w, 0.0)
        pv = jnp.dot(w.astype(BF16), vb, preferred_element_type=F32)
        return pv, tail[:, :1] + log_1m[:, :1]

    def own_lanes(pvs):
        out = pvs[-1]
        for h in range(heads - 2, -1, -1):
            out = jnp.where(head_of_lane == h, pvs[h], out)
        return out

    has_prev = (i > 0).astype(F32)
    accs, runs = [], []
    for g in range(n_groups):
        kb0, vb0 = load_kv(i, g)
        kb1, vb1 = load_kv(jnp.maximum(i - 1, 0), g)
        pvs = []
        for h in range(heads):
            pv0, tot0 = head_tile(qh[g][h], kb0, vb0, None, True)
            pv1, tot1 = head_tile(qh[g][h], kb1, vb1, tot0, False)
            pvs.append(pv0 + has_prev * pv1)
            runs.append(tot0 + has_prev * tot1)
        accs.append(own_lanes(pvs))

    def live(runs):
        top = runs[0]
        for r in runs[1:]:
            top = jnp.maximum(top, r)
        return jnp.max(top) > ZERO_WEIGHT_LOG2

    def cond(state):
        jj, alive = state[0], state[1]
        return jnp.logical_and(jj <= i, alive)

    def body(state):
        jj, accs, runs = state[0], state[2], state[3]
        new_accs, new_runs = [], []
        for g in range(n_groups):
            kb, vb = load_kv(i - jj, g)
            pvs = []
            for h in range(heads):
                pv, tot = head_tile(qh[g][h], kb, vb, runs[g * heads + h], False)
                pvs.append(pv)
                new_runs.append(runs[g * heads + h] + tot)
            new_accs.append(accs[g] + own_lanes(pvs))
        return (jj + 1, live(new_runs), new_accs, new_runs)

    state = lax.while_loop(cond, body, (jnp.int32(2), live(runs), accs, runs))
    for g in range(n_groups):
        o_ref[0, :, lanes(g)] = state[2][g].astype(BF16)


def _sbattn(q, k, v, blk=256):
    bsz, seqlen, width = q.shape
    qspec = pl.BlockSpec((1, blk, width), lambda b, i: (b, i, 0))
    kvspec = pl.BlockSpec((1, seqlen, width), lambda b, i: (b, 0, 0))
    return pl.pallas_call(
        functools.partial(_attn_kernel, blk=blk),
        grid=(bsz, seqlen // blk),
        in_specs=[qspec, kvspec, kvspec],
        out_specs=qspec,
        out_shape=jax.ShapeDtypeStruct((bsz, seqlen, width), BF16),
        compiler_params=pltpu.CompilerParams(
            dimension_semantics=("parallel", "arbitrary"), vmem_limit_bytes=VMEM_LIMIT),
        name="sbattn",
    )(q, k, v)


FF_CHUNK = 1024


def _mlp_ple(h, p_ref, lnm_ref, wup_ref, wdn_ref, lnp_ref, wg_ref, wpu_ref, o_ref):
    hn = _rms(h, lnm_ref[...]).astype(BF16)
    acc = h
    for c in range(0, D_FF, FF_CHUNK):
        a = jnp.dot(hn, wup_ref[:, c:c + FF_CHUNK], preferred_element_type=F32)
        a = jnp.square(jnp.maximum(a, 0.0)).astype(BF16)
        acc = acc + jnp.dot(a, wdn_ref[c:c + FF_CHUNK, :], preferred_element_type=F32)
    gate = jax.nn.sigmoid(jnp.dot(_rms(acc, lnp_ref[...]).astype(BF16), wg_ref[...], preferred_element_type=F32))
    pe = jnp.dot(p_ref[0].astype(BF16), wpu_ref[...], preferred_element_type=F32)
    o_ref[0] = acc + pe * gate


def _layer_even_kernel(x_ref, s5_ref, sb_ref, wout_ref, p_ref, lnm_ref, wup_ref, wdn_ref, lnp_ref, wg_ref,
                       wpu_ref, o_ref):
    mixed = (jnp.dot(s5_ref[0], wout_ref[:S5_WIDTH, :], preferred_element_type=F32)
             + jnp.dot(sb_ref[0], wout_ref[S5_WIDTH:, :], preferred_element_type=F32))
    _mlp_ple(x_ref[0] + mixed, p_ref, lnm_ref, wup_ref, wdn_ref, lnp_ref, wg_ref, wpu_ref, o_ref)


def _layer_odd_kernel(h_ref, halo_ref, lno_ref, pw_ref, ps_ref, p_ref, lnm_ref, wup_ref, wdn_ref, lnp_ref,
                      wg_ref, wpu_ref, o_ref, *, tm):
    i = pl.program_id(1)
    h = h_ref[0]
    hn = _rms(h, lno_ref[...])
    halo = _rms(halo_ref[0], lno_ref[...]) * (i > 0).astype(F32)
    ext = jnp.concatenate([halo, hn], axis=0)
    t = lax.broadcasted_iota(jnp.int32, (tm, 1), 0) + i * tm
    outs = []
    for g, window in enumerate(POOL_WINDOWS):
        sl = slice(g * POOL_GROUP, (g + 1) * POOL_GROUP)
        s = ext[:, sl]
        span = 1
        while span < window:
            s = s + jnp.concatenate([jnp.zeros((span, POOL_GROUP), F32), s[:-span]], axis=0)
            span *= 2
        count = jnp.minimum(t + 1, window).astype(F32)
        y = s[POOL_HALO:] / count - hn[:, sl]
        outs.append(jnp.dot(y.astype(BF16), pw_ref[g], preferred_element_type=F32))
    mixed = jnp.concatenate(outs, axis=1) * ps_ref[...]
    _mlp_ple(h + mixed, p_ref, lnm_ref, wup_ref, wdn_ref, lnp_ref, wg_ref, wpu_ref, o_ref)


def _tail_specs(tm, layer):
    return [
        pl.BlockSpec((None, 1, tm, PLE_DIM), lambda b, i: (layer, b, i, 0)),
        _const_spec((1, D_MODEL)), _const_spec((D_MODEL, D_FF)), _const_spec((D_FF, D_MODEL)),
        _const_spec((1, D_MODEL)), _const_spec((D_MODEL, D_MODEL)), _const_spec((PLE_DIM, D_MODEL)),
    ]


def _tail_args(p, ln_mlp, w_up, w_down, ln_ple, w_gate, w_ple_up):
    return (p, ln_mlp.astype(F32)[None, :], w_up.astype(BF16), w_down.astype(BF16),
            ln_ple.astype(F32)[None, :], w_gate.astype(BF16), w_ple_up.astype(BF16))


def _layer_even(x, s5, sb, w_out, tail, layer, tm=512):
    bsz, seqlen, _ = x.shape
    row = lambda w: pl.BlockSpec((1, tm, w), lambda b, i: (b, i, 0))
    return pl.pallas_call(
        _layer_even_kernel,
        grid=(bsz, seqlen // tm),
        in_specs=[row(D_MODEL),
                  row(S5_WIDTH),
                  row(SB_WIDTH),
                  _const_spec((D_MODEL, D_MODEL))] + _tail_specs(tm, layer),
        out_specs=row(D_MODEL),
        out_shape=jax.ShapeDtypeStruct(x.shape, F32),
        compiler_params=pltpu.CompilerParams(
            dimension_semantics=("parallel", "parallel"), vmem_limit_bytes=VMEM_LIMIT),
        name="layer_even",
    )(x, s5, sb, w_out.astype(BF16), *tail)


def _layer_odd(h, ln_odd, pool_w, pool_scale, tail, layer, tm=512):
    bsz, seqlen, _ = h.shape
    row = lambda w: pl.BlockSpec((1, tm, w), lambda b, i: (b, i, 0))
    halo_blocks = tm // POOL_HALO
    return pl.pallas_call(
        functools.partial(_layer_odd_kernel, tm=tm),
        grid=(bsz, seqlen // tm),
        in_specs=[row(D_MODEL),
                  pl.BlockSpec((1, POOL_HALO, D_MODEL), lambda b, i: (b, jnp.maximum(i * halo_blocks - 1, 0), 0)),
                  _const_spec((1, D_MODEL)),
                  _const_spec((len(POOL_WINDOWS), POOL_GROUP, POOL_GROUP)),
                  _const_spec((1, D_MODEL))] + _tail_specs(tm, layer),
        out_specs=row(D_MODEL),
        out_shape=jax.ShapeDtypeStruct(h.shape, F32),
        compiler_params=pltpu.CompilerParams(
            dimension_semantics=("parallel", "parallel"), vmem_limit_bytes=VMEM_LIMIT),
        name="layer_odd",
    )(h, h, ln_odd.astype(F32)[None, :], pool_w.astype(BF16), pool_scale.astype(F32)[None, :], *tail)


def kernel(x, p, ln_mix_even, w_in_even, s5_lambda_re, s5_lambda_im, s5_log_dt, s5_b_re, s5_b_im, s5_c_re,
           s5_c_im, s5_d, s5_w_glu, sb_q_gain, sb_k_gain, w_out_even, ln_mix_odd, pool_w, pool_scale, ln_mlp,
           w_mlp_up, w_mlp_down, ln_ple, w_ple_gate, w_ple_up):
    h = x
    for i in range(p.shape[0]):
        j = i // 2
        tail = _tail_args(p, ln_mlp[i], w_mlp_up[i], w_mlp_down[i], ln_ple[i], w_ple_gate[i], w_ple_up[i])
        if i % 2 == 0:
            u, q, k, v = _inproj(h, ln_mix_even[j], w_in_even[j], sb_q_gain[j], sb_k_gain[j])
            s5 = _s5(u, s5_lambda_re[j], s5_lambda_im[j], s5_log_dt[j], s5_b_re[j], s5_b_im[j],
                     s5_c_re[j], s5_c_im[j], s5_d[j], s5_w_glu[j])
            sb = _sbattn(q, k, v)
            h = _layer_even(h, s5, sb, w_out_even[j], tail, i)
        else:
            h = _layer_odd(h, ln_mix_odd[j], pool_w[j], pool_scale[j], tail, i)
    return h
```

```python
import functools
import math

import jax
import jax.numpy as jnp
from jax import lax
from jax.experimental import pallas as pl
from jax.experimental.pallas import tpu as pltpu

F32 = jnp.float32
BF16 = jnp.bfloat16

D_MODEL = 1024
S5_WIDTH = 512
S5_GROUP = 16
S5_GROUPS = 32
S5_STATE = 64
SB_HEAD_DIM = 64
SB_WIDTH = 512
IN_WIDTH = S5_WIDTH + 3 * SB_WIDTH
POOL_WINDOWS = (2, 4, 8, 16)
POOL_GROUP = 256
POOL_HALO = 16
D_FF = 4 * D_MODEL
PLE_DIM = 256
EPS = 1e-6
LOG2_E = math.log2(math.e)
ZERO_WEIGHT_LOG2 = -150.0

LANES = 128
SUBLANES = 8
STATE_LANES = 2 * S5_GROUPS * S5_STATE
STATE_HALF = STATE_LANES // 2
COL_TILE = 2 * LANES

VMEM_LIMIT = 56 * 1024 * 1024


def _const_spec(shape):
    nd = len(shape)
    return pl.BlockSpec(shape, lambda *_: (0,) * nd, pipeline_mode=pl.Buffered(1))


def _rms(x, gain):
    ms = jnp.mean(x * x, axis=-1, keepdims=True)
    return x * lax.rsqrt(ms + EPS) * gain


def _inproj_kernel(x_ref, ln_ref, w_ref, qg_ref, kg_ref, hm_ref, u_ref, q_ref, k_ref, v_ref):
    hn = _rms(x_ref[0], ln_ref[...]).astype(BF16)
    proj = jnp.dot(hn, w_ref[...], preferred_element_type=F32)
    u_ref[0] = proj[:, :S5_WIDTH].astype(BF16)
    hm = hm_ref[...]

    def head_norm(t, gain):
        ms = jnp.dot((t * t).astype(BF16), hm, preferred_element_type=F32)
        return t * lax.rsqrt(ms + EPS) * gain

    q = proj[:, S5_WIDTH:S5_WIDTH + SB_WIDTH]
    k = proj[:, S5_WIDTH + SB_WIDTH:S5_WIDTH + 2 * SB_WIDTH]
    q_ref[0] = (head_norm(q, qg_ref[...]) * (SB_HEAD_DIM ** -0.5 * LOG2_E)).astype(BF16)
    k_ref[0] = head_norm(k, kg_ref[...]).astype(BF16)
    v_ref[0] = proj[:, S5_WIDTH + 2 * SB_WIDTH:].astype(BF16)


def _inproj(x, ln, w_in, q_gain, k_gain, tl=1024):
    bsz, seqlen, _ = x.shape
    heads = SB_WIDTH // SB_HEAD_DIM
    hm = jnp.kron(jnp.eye(heads, dtype=F32), jnp.full((SB_HEAD_DIM, SB_HEAD_DIM), 1.0 / SB_HEAD_DIM, F32)).astype(BF16)
    qg = jnp.tile(q_gain.astype(F32), heads)[None, :]
    kg = jnp.tile(k_gain.astype(F32), heads)[None, :]
    act = lambda: pl.BlockSpec((1, tl, SB_WIDTH), lambda b, i: (b, i, 0))
    return pl.pallas_call(
        _inproj_kernel,
        grid=(bsz, seqlen // tl),
        in_specs=[
            pl.BlockSpec((1, tl, D_MODEL), lambda b, i: (b, i, 0)),
            _const_spec((1, D_MODEL)),
            _const_spec((D_MODEL, IN_WIDTH)),
            _const_spec((1, SB_WIDTH)),
            _const_spec((1, SB_WIDTH)),
            _const_spec((SB_WIDTH, SB_WIDTH)),
        ],
        out_specs=[act(), act(), act(), act()],
        out_shape=[
            jax.ShapeDtypeStruct((bsz, seqlen, S5_WIDTH), BF16),
            jax.ShapeDtypeStruct((bsz, seqlen, SB_WIDTH), BF16),
            jax.ShapeDtypeStruct((bsz, seqlen, SB_WIDTH), BF16),
            jax.ShapeDtypeStruct((bsz, seqlen, SB_WIDTH), BF16),
        ],
        compiler_params=pltpu.CompilerParams(
            dimension_semantics=("parallel", "parallel"), vmem_limit_bytes=VMEM_LIMIT),
        name="inproj",
    )(x, ln.astype(F32)[None, :], w_in.astype(BF16), qg, kg, hm)


S5_PERM_STEPS = 32


def _s5_kernel(u_ref, bw_ref, cw_ref, ar_ref, ai_ref, d_ref, wglu_ref, y_ref, st_ref, *, steps):
    @pl.when(pl.program_id(0) == 0)
    def _():
        st_ref[...] = jnp.zeros_like(st_ref)

    prow = S5_PERM_STEPS * SUBLANES
    r = lax.broadcasted_iota(jnp.int32, (prow, prow), 0)
    c = lax.broadcasted_iota(jnp.int32, (prow, prow), 1)
    to_tb = jnp.logical_and(r // SUBLANES == c % S5_PERM_STEPS, r % SUBLANES == c // S5_PERM_STEPS).astype(BF16)
    to_bt = jnp.logical_and(c // SUBLANES == r % S5_PERM_STEPS, c % SUBLANES == r // S5_PERM_STEPS).astype(BF16)
    u = jnp.concatenate(
        [jnp.dot(to_tb, u_ref[:, t0:t0 + S5_PERM_STEPS, :].reshape(prow, S5_WIDTH), preferred_element_type=F32)
         for t0 in range(0, steps, S5_PERM_STEPS)], axis=0)
    ub = u.astype(BF16)
    half_ch = S5_WIDTH // 2
    tiles_per_half = STATE_HALF // COL_TILE
    parts = []
    for kt in range(2):
        bu = jnp.dot(ub[:, kt * half_ch:(kt + 1) * half_ch], bw_ref[kt], preferred_element_type=F32)
        cols = []
        for cl in range(tiles_per_half):
            c = kt * tiles_per_half + cl
            ar = jnp.broadcast_to(ar_ref[:, c * LANES:(c + 1) * LANES], (SUBLANES, LANES))
            ai = jnp.broadcast_to(ai_ref[:, c * LANES:(c + 1) * LANES], (SUBLANES, LANES))
            xr = st_ref[:, c * COL_TILE:c * COL_TILE + LANES]
            xi = st_ref[:, c * COL_TILE + LANES:(c + 1) * COL_TILE]
            res_r, res_i = [], []
            for t in range(steps):
                rs = slice(t * SUBLANES, (t + 1) * SUBLANES)
                bur = bu[rs, cl * COL_TILE:cl * COL_TILE + LANES]
                bui = bu[rs, cl * COL_TILE + LANES:(cl + 1) * COL_TILE]
                xr, xi = ar * xr - ai * xi + bur, ar * xi + ai * xr + bui
                res_r.append(xr)
                res_i.append(xi)
            st_ref[:, c * COL_TILE:c * COL_TILE + LANES] = xr
            st_ref[:, c * COL_TILE + LANES:(c + 1) * COL_TILE] = xi
            cols += [jnp.concatenate(res_r, axis=0), jnp.concatenate(res_i, axis=0)]
        xh = jnp.concatenate(cols, axis=1).astype(BF16)
        parts.append(jnp.dot(xh, cw_ref[kt], preferred_element_type=F32))
    y = jnp.concatenate(parts, axis=1)
    y = y + d_ref[...] * u
    y = jax.nn.gelu(y)
    g = jnp.dot(y.astype(BF16), wglu_ref[...], preferred_element_type=F32)
    y = (y * jax.nn.sigmoid(g)).astype(BF16)
    for n, t0 in enumerate(range(0, steps, S5_PERM_STEPS)):
        y_bt = jnp.dot(to_bt, y[n * prow:(n + 1) * prow], preferred_element_type=F32).astype(BF16)
        y_ref[:, t0:t0 + S5_PERM_STEPS, :] = y_bt.reshape(SUBLANES, S5_PERM_STEPS, S5_WIDTH)


def _s5_params(lam_re, lam_im, log_dt, b_re, b_im, c_re, c_im):
    lr = lam_re.astype(F32)
    li = lam_im.astype(F32)
    dt = jnp.exp(log_dt.astype(F32))[:, None]
    mag = jnp.exp(lr * dt)
    abar_r = mag * jnp.cos(li * dt)
    abar_i = mag * jnp.sin(li * dt)
    den = lr * lr + li * li
    nr = abar_r - 1.0
    ni = abar_i
    fr = (nr * lr + ni * li) / den
    fi = (ni * lr - nr * li) / den
    br = b_re.astype(F32)
    bi = b_im.astype(F32)
    bbar_r = fr[..., None] * br - fi[..., None] * bi
    bbar_i = fr[..., None] * bi + fi[..., None] * br
    half_ch = S5_WIDTH // 2
    rp = 2 * S5_STATE
    a_b = jnp.stack([bbar_r, bbar_i], axis=1).transpose(0, 3, 1, 2).reshape(2, half_ch, rp)
    a_c = (jnp.stack([c_re.astype(F32), -c_im.astype(F32)], axis=1).transpose(1, 3, 0, 2)
           .reshape(rp, 2, half_ch).transpose(1, 0, 2))
    lane = jnp.arange(STATE_HALF)
    lane_rp = ((lane // LANES) % 2) * S5_STATE + lane % S5_STATE
    lane_group = 2 * (lane // COL_TILE) + (lane // S5_STATE) % 2
    sel = (jnp.arange(rp)[:, None] == lane_rp[None, :]).astype(F32)
    mask = (jnp.arange(half_ch) // S5_GROUP)[:, None] == lane_group[None, :]
    exact = lax.Precision.HIGHEST
    bw = jnp.where(mask, jnp.einsum('krc,cl->krl', a_b, sel, precision=exact), 0.0).astype(BF16)
    cw = jnp.where(mask.T, jnp.einsum('lc,kcr->klr', sel.T, a_c, precision=exact), 0.0).astype(BF16)
    a_r = abar_r.reshape(1, S5_GROUPS * S5_STATE)
    a_i = abar_i.reshape(1, S5_GROUPS * S5_STATE)
    return bw, cw, a_r, a_i


def _s5(u, lam_re, lam_im, log_dt, b_re, b_im, c_re, c_im, d, w_glu, steps=128):
    bsz, seqlen, _ = u.shape
    assert bsz == SUBLANES
    blk = pl.BlockSpec((bsz, steps, S5_WIDTH), lambda c: (0, c, 0))
    bw, cw, a_r, a_i = _s5_params(lam_re, lam_im, log_dt, b_re, b_im, c_re, c_im)
    return pl.pallas_call(
        functools.partial(_s5_kernel, steps=steps),
        grid=(seqlen // steps,),
        in_specs=[
            blk,
            _const_spec(bw.shape), _const_spec(cw.shape),
            _const_spec(a_r.shape), _const_spec(a_i.shape),
            _const_spec((1, S5_WIDTH)), _const_spec((S5_WIDTH, S5_WIDTH)),
        ],
        out_specs=blk,
        out_shape=jax.ShapeDtypeStruct(u.shape, BF16),
        scratch_shapes=[pltpu.VMEM((SUBLANES, STATE_LANES), F32)],
        compiler_params=pltpu.CompilerParams(
            dimension_semantics=("arbitrary",), vmem_limit_bytes=VMEM_LIMIT),
        name="s5",
    )(u, bw, cw, a_r, a_i, d.astype(F32)[None, :], w_glu.astype(BF16))


ATTN_GROUP_LANES = 256


def _attn_kernel(q_ref, k_ref, v_ref, o_ref, *, blk):
    i = pl.program_id(1)
    n_groups = q_ref.shape[-1] // ATTN_GROUP_LANES
    heads = ATTN_GROUP_LANES // SB_HEAD_DIM
    head_of_lane = lax.broadcasted_iota(jnp.int32, (blk, ATTN_GROUP_LANES), 1) // SB_HEAD_DIM
    row = lax.broadcasted_iota(jnp.int32, (blk, blk), 0)
    col = lax.broadcasted_iota(jnp.int32, (blk, blk), 1)
    later = (row > col).astype(BF16)
    causal = col < row

    def lanes(g):
        return slice(g * ATTN_GROUP_LANES, (g + 1) * ATTN_GROUP_LANES)

    def load_kv(j, g):
        k0 = pl.multiple_of(j * blk, blk)
        return k_ref[0, pl.ds(k0, blk), lanes(g)], v_ref[0, pl.ds(k0, blk), lanes(g)]

    q_rows = []
    for g in range(n_groups):
        qg = q_ref[0, :, lanes(g)]
        q_rows.append(jnp.concatenate(
            [jnp.where(head_of_lane == h, qg, jnp.zeros_like(qg)) for h in range(heads)], axis=0))
    causal_rows = jnp.concatenate([causal] * heads, axis=0)

    def logits(q_all, kb, diag):
        z = lax.dot_general(q_all, kb, (((1,), (1,)), ((), ())), preferred_element_type=F32)
        sp = jnp.log2(1.0 + jnp.exp2(-jnp.abs(z)))
        log_beta = jnp.minimum(z, 0.0) - sp
        log_1m = log_beta - z
        if diag:
            log_1m = jnp.where(causal_rows, log_1m, 0.0)
        return log_beta, log_1m

    def weighted_values(log_beta, log_1m, tail, vb, run, diag):
        expo = log_beta + tail
        if run is not None:
            expo = expo + run
        w = jnp.exp2(expo)
        if diag:
            w = jnp.where(causal_rows, w, 0.0)
        pv = jnp.dot(w.astype(BF16), vb, preferred_element_type=F32)
        out = pv[(heads - 1) * blk:]
        for h in range(heads - 2, -1, -1):
            out = jnp.where(head_of_lane == h, pv[h * blk:(h + 1) * blk], out)
        return out, tail[:, :1] + log_1m[:, :1]

    def group_tile(q_all, kb, vb, run, diag):
        log_beta, log_1m = logits(q_all, kb, diag)
        tail = jnp.dot(log_1m.astype(BF16), later, preferred_element_type=F32)
        return weighted_values(log_beta, log_1m, tail, vb, run, diag)

    has_prev = (i > 0).astype(F32)
    rows = heads * blk
    parts = []
    for g in range(n_groups):
        kb0, vb0 = load_kv(i, g)
        kb1, vb1 = load_kv(jnp.maximum(i - 1, 0), g)
        parts += [(logits(q_rows[g], kb0, True), vb0), (logits(q_rows[g], kb1, False), vb1)]
    tails = jnp.dot(jnp.concatenate([lg[1] for lg, _ in parts], axis=0).astype(BF16), later,
                    preferred_element_type=F32)
    accs, runs = [], []
    for g in range(n_groups):
        (lb0, l10), vb0 = parts[2 * g]
        (lb1, l11), vb1 = parts[2 * g + 1]
        pv0, tot0 = weighted_values(lb0, l10, tails[(2 * g) * rows:(2 * g + 1) * rows], vb0, None, True)
        pv1, tot1 = weighted_values(lb1, l11, tails[(2 * g + 1) * rows:(2 * g + 2) * rows], vb1, tot0, False)
        accs.append(pv0 + has_prev * pv1)
        runs.append(tot0 + has_prev * tot1)

    def live(runs):
        top = runs[0]
        for r in runs[1:]:
            top = jnp.maximum(top, r)
        return jnp.max(top) > ZERO_WEIGHT_LOG2

    def cond(state):
        jj, alive = state[0], state[1]
        return jnp.logical_and(jj <= i, alive)

    def body(state):
        jj, accs, runs = state[0], state[2], state[3]
        new_accs, new_runs = [], []
        for g in range(n_groups):
            kb, vb = load_kv(i - jj, g)
            pv, tot = group_tile(q_rows[g], kb, vb, runs[g], False)
            new_accs.append(accs[g] + pv)
            new_runs.append(runs[g] + tot)
        return (jj + 1, live(new_runs), new_accs, new_runs)

    state = lax.while_loop(cond, body, (jnp.int32(2), live(runs), accs, runs))
    for g in range(n_groups):
        o_ref[0, :, lanes(g)] = state[2][g].astype(BF16)


def _sbattn(q, k, v, blk=256):
    bsz, seqlen, width = q.shape
    qspec = pl.BlockSpec((1, blk, width), lambda b, i: (b, i, 0))
    kvspec = pl.BlockSpec((1, seqlen, width), lambda b, i: (b, 0, 0))
    return pl.pallas_call(
        functools.partial(_attn_kernel, blk=blk),
        grid=(bsz, seqlen // blk),
        in_specs=[qspec, kvspec, kvspec],
        out_specs=qspec,
        out_shape=jax.ShapeDtypeStruct((bsz, seqlen, width), BF16),
        compiler_params=pltpu.CompilerParams(
            dimension_semantics=("parallel", "arbitrary"), vmem_limit_bytes=VMEM_LIMIT),
        name="sbattn",
    )(q, k, v)


FF_CHUNK = 1024


def _mlp_ple(h, p_ref, lnm_ref, wup_ref, wdn_ref, lnp_ref, wg_ref, wpu_ref, o_ref):
    hn = _rms(h, lnm_ref[...]).astype(BF16)
    acc = h
    for c in range(0, D_FF, FF_CHUNK):
        a = jnp.dot(hn, wup_ref[:, c:c + FF_CHUNK], preferred_element_type=F32)
        a = jnp.square(jnp.maximum(a, 0.0)).astype(BF16)
        acc = acc + jnp.dot(a, wdn_ref[c:c + FF_CHUNK, :], preferred_element_type=F32)
    gate = jax.nn.sigmoid(jnp.dot(_rms(acc, lnp_ref[...]).astype(BF16), wg_ref[...], preferred_element_type=F32))
    pe = jnp.dot(p_ref[0].astype(BF16), wpu_ref[...], preferred_element_type=F32)
    o_ref[0] = acc + pe * gate


def _layer_even_kernel(x_ref, s5_ref, sb_ref, wout_ref, p_ref, lnm_ref, wup_ref, wdn_ref, lnp_ref, wg_ref,
                       wpu_ref, o_ref):
    mixed = (jnp.dot(s5_ref[0], wout_ref[:S5_WIDTH, :], preferred_element_type=F32)
             + jnp.dot(sb_ref[0], wout_ref[S5_WIDTH:, :], preferred_element_type=F32))
    _mlp_ple(x_ref[0] + mixed, p_ref, lnm_ref, wup_ref, wdn_ref, lnp_ref, wg_ref, wpu_ref, o_ref)


def _layer_odd_kernel(h_ref, halo_ref, lno_ref, pw_ref, ps_ref, p_ref, lnm_ref, wup_ref, wdn_ref, lnp_ref,
                      wg_ref, wpu_ref, o_ref, *, tm):
    i = pl.program_id(1)
    h = h_ref[0]
    hn = _rms(h, lno_ref[...])
    halo = _rms(halo_ref[0], lno_ref[...]) * (i > 0).astype(F32)
    ext = jnp.concatenate([halo, hn], axis=0)
    t = lax.broadcasted_iota(jnp.int32, (tm, 1), 0) + i * tm
    outs = []
    for g, window in enumerate(POOL_WINDOWS):
        sl = slice(g * POOL_GROUP, (g + 1) * POOL_GROUP)
        s = ext[:, sl]
        span = 1
        while span < window:
            s = s + jnp.concatenate([jnp.zeros((span, POOL_GROUP), F32), s[:-span]], axis=0)
            span *= 2
        count = jnp.minimum(t + 1, window).astype(F32)
        y = s[POOL_HALO:] / count - hn[:, sl]
        outs.append(jnp.dot(y.astype(BF16), pw_ref[g], preferred_element_type=F32))
    mixed = jnp.concatenate(outs, axis=1) * ps_ref[...]
    _mlp_ple(h + mixed, p_ref, lnm_ref, wup_ref, wdn_ref, lnp_ref, wg_ref, wpu_ref, o_ref)


def _tail_specs(tm, layer):
    return [
        pl.BlockSpec((None, 1, tm, PLE_DIM), lambda b, i: (layer, b, i, 0)),
        _const_spec((1, D_MODEL)), _const_spec((D_MODEL, D_FF)), _const_spec((D_FF, D_MODEL)),
        _const_spec((1, D_MODEL)), _const_spec((D_MODEL, D_MODEL)), _const_spec((PLE_DIM, D_MODEL)),
    ]


def _tail_args(p, ln_mlp, w_up, w_down, ln_ple, w_gate, w_ple_up):
    return (p, ln_mlp.astype(F32)[None, :], w_up.astype(BF16), w_down.astype(BF16),
            ln_ple.astype(F32)[None, :], w_gate.astype(BF16), w_ple_up.astype(BF16))


def _layer_even(x, s5, sb, w_out, tail, layer, tm=512):
    bsz, seqlen, _ = x.shape
    row = lambda w: pl.BlockSpec((1, tm, w), lambda b, i: (b, i, 0))
    return pl.pallas_call(
        _layer_even_kernel,
        grid=(bsz, seqlen // tm),
        in_specs=[row(D_MODEL),
                  row(S5_WIDTH),
                  row(SB_WIDTH),
                  _const_spec((D_MODEL, D_MODEL))] + _tail_specs(tm, layer),
        out_specs=row(D_MODEL),
        out_shape=jax.ShapeDtypeStruct(x.shape, F32),
        compiler_params=pltpu.CompilerParams(
            dimension_semantics=("parallel", "parallel"), vmem_limit_bytes=VMEM_LIMIT),
        name="layer_even",
    )(x, s5, sb, w_out.astype(BF16), *tail)


def _layer_odd(h, ln_odd, pool_w, pool_scale, tail, layer, tm=512):
    bsz, seqlen, _ = h.shape
    row = lambda w: pl.BlockSpec((1, tm, w), lambda b, i: (b, i, 0))
    halo_blocks = tm // POOL_HALO
    return pl.pallas_call(
        functools.partial(_layer_odd_kernel, tm=tm),
        grid=(bsz, seqlen // tm),
        in_specs=[row(D_MODEL),
                  pl.BlockSpec((1, POOL_HALO, D_MODEL), lambda b, i: (b, jnp.maximum(i * halo_blocks - 1, 0), 0)),
                  _const_spec((1, D_MODEL)),
                  _const_spec((len(POOL_WINDOWS), POOL_GROUP, POOL_GROUP)),
                  _const_spec((1, D_MODEL))] + _tail_specs(tm, layer),
        out_specs=row(D_MODEL),
        out_shape=jax.ShapeDtypeStruct(h.shape, F32),
        compiler_params=pltpu.CompilerParams(
            dimension_semantics=("parallel", "parallel"), vmem_limit_bytes=VMEM_LIMIT),
        name="layer_odd",
    )(h, h, ln_odd.astype(F32)[None, :], pool_w.astype(BF16), pool_scale.astype(F32)[None, :], *tail)


def kernel(x, p, ln_mix_even, w_in_even, s5_lambda_re, s5_lambda_im, s5_log_dt, s5_b_re, s5_b_im, s5_c_re,
           s5_c_im, s5_d, s5_w_glu, sb_q_gain, sb_k_gain, w_out_even, ln_mix_odd, pool_w, pool_scale, ln_mlp,
           w_mlp_up, w_mlp_down, ln_ple, w_ple_gate, w_ple_up):
    h = x
    for i in range(p.shape[0]):
        j = i // 2
        tail = _tail_args(p, ln_mlp[i], w_mlp_up[i], w_mlp_down[i], ln_ple[i], w_ple_gate[i], w_ple_up[i])
        if i % 2 == 0:
            u, q, k, v = _inproj(h, ln_mix_even[j], w_in_even[j], sb_q_gain[j], sb_k_gain[j])
            s5 = _s5(u, s5_lambda_re[j], s5_lambda_im[j], s5_log_dt[j], s5_b_re[j], s5_b_im[j],
                     s5_c_re[j], s5_c_im[j], s5_d[j], s5_w_glu[j])
            sb = _sbattn(q, k, v)
            h = _layer_even(h, s5, sb, w_out_even[j], tail, i)
        else:
            h = _layer_odd(h, ln_mix_odd[j], pool_w[j], pool_scale[j], tail, i)
    return h
```

```python
import functools
import math

import jax
import jax.numpy as jnp
from jax import lax
from jax.experimental import pallas as pl
from jax.experimental.pallas import tpu as pltpu

F32 = jnp.float32
BF16 = jnp.bfloat16

D_MODEL = 1024
S5_WIDTH = 512
S5_GROUP = 16
S5_GROUPS = 32
S5_STATE = 64
SB_HEAD_DIM = 64
SB_WIDTH = 512
IN_WIDTH = S5_WIDTH + 3 * SB_WIDTH
POOL_WINDOWS = (2, 4, 8, 16)
POOL_GROUP = 256
POOL_HALO = max(POOL_WINDOWS)
D_FF = 4 * D_MODEL
PLE_DIM = 256
EPS = 1e-6
LOG2_E = math.log2(math.e)
ZERO_WEIGHT_LOG2 = -150.0

LANES = 128
SUBLANES = 8
STATE_LANES = 2 * S5_GROUPS * S5_STATE
STATE_HALF = STATE_LANES // 2
COL_TILE = 2 * LANES

V7X_VMEM_BYTES = 64 * 1024 * 1024
VMEM_LIMIT = V7X_VMEM_BYTES * 7 // 8


def _const_spec(shape):
    nd = len(shape)
    return pl.BlockSpec(shape, lambda *_: (0,) * nd, pipeline_mode=pl.Buffered(1))


def _rms(x, gain):
    ms = jnp.mean(x * x, axis=-1, keepdims=True)
    return x * lax.rsqrt(ms + EPS) * gain


def _inproj_kernel(x_ref, ln_ref, w_ref, qg_ref, kg_ref, hm_ref, u_ref, q_ref, k_ref, v_ref):
    hn = _rms(x_ref[0], ln_ref[...]).astype(BF16)
    proj = jnp.dot(hn, w_ref[...], preferred_element_type=F32)
    u_ref[0] = proj[:, :S5_WIDTH].astype(BF16)
    hm = hm_ref[...]

    def head_norm(t, gain):
        ms = jnp.dot((t * t).astype(BF16), hm, preferred_element_type=F32)
        return t * lax.rsqrt(ms + EPS) * gain

    q = proj[:, S5_WIDTH:S5_WIDTH + SB_WIDTH]
    k = proj[:, S5_WIDTH + SB_WIDTH:S5_WIDTH + 2 * SB_WIDTH]
    q_ref[0] = (head_norm(q, qg_ref[...]) * (SB_HEAD_DIM ** -0.5 * LOG2_E)).astype(BF16)
    k_ref[0] = head_norm(k, kg_ref[...]).astype(BF16)
    v_ref[0] = proj[:, S5_WIDTH + 2 * SB_WIDTH:].astype(BF16)


def _inproj(x, ln, w_in, q_gain, k_gain, tl=1024):
    bsz, seqlen, _ = x.shape
    heads = SB_WIDTH // SB_HEAD_DIM
    hm = jnp.kron(jnp.eye(heads, dtype=F32), jnp.full((SB_HEAD_DIM, SB_HEAD_DIM), 1.0 / SB_HEAD_DIM, F32)).astype(BF16)
    qg = jnp.tile(q_gain.astype(F32), heads)[None, :]
    kg = jnp.tile(k_gain.astype(F32), heads)[None, :]
    act = lambda: pl.BlockSpec((1, tl, SB_WIDTH), lambda b, i: (b, i, 0))
    return pl.pallas_call(
        _inproj_kernel,
        grid=(bsz, seqlen // tl),
        in_specs=[
            pl.BlockSpec((1, tl, D_MODEL), lambda b, i: (b, i, 0)),
            _const_spec((1, D_MODEL)),
            _const_spec((D_MODEL, IN_WIDTH)),
            _const_spec((1, SB_WIDTH)),
            _const_spec((1, SB_WIDTH)),
            _const_spec((SB_WIDTH, SB_WIDTH)),
        ],
        out_specs=[act(), act(), act(), act()],
        out_shape=[
            jax.ShapeDtypeStruct((bsz, seqlen, S5_WIDTH), BF16),
            jax.ShapeDtypeStruct((bsz, seqlen, SB_WIDTH), BF16),
            jax.ShapeDtypeStruct((bsz, seqlen, SB_WIDTH), BF16),
            jax.ShapeDtypeStruct((bsz, seqlen, SB_WIDTH), BF16),
        ],
        compiler_params=pltpu.CompilerParams(
            dimension_semantics=("parallel", "parallel"), vmem_limit_bytes=VMEM_LIMIT),
        name="inproj",
    )(x, ln.astype(F32)[None, :], w_in.astype(BF16), qg, kg, hm)


S5_PERM_STEPS = 32


def _s5_kernel(u_ref, bw_ref, cw_ref, ar_ref, ai_ref, d_ref, wglu_ref, y_ref, st_ref, *, steps):
    @pl.when(pl.program_id(0) == 0)
    def _():
        st_ref[...] = jnp.zeros_like(st_ref)

    prow = S5_PERM_STEPS * SUBLANES
    r = lax.broadcasted_iota(jnp.int32, (prow, prow), 0)
    c = lax.broadcasted_iota(jnp.int32, (prow, prow), 1)
    to_tb = jnp.logical_and(r // SUBLANES == c % S5_PERM_STEPS, r % SUBLANES == c // S5_PERM_STEPS).astype(BF16)
    to_bt = jnp.logical_and(c // SUBLANES == r % S5_PERM_STEPS, c % SUBLANES == r // S5_PERM_STEPS).astype(BF16)
    u = jnp.concatenate(
        [jnp.dot(to_tb, u_ref[:, t0:t0 + S5_PERM_STEPS, :].reshape(prow, S5_WIDTH), preferred_element_type=F32)
         for t0 in range(0, steps, S5_PERM_STEPS)], axis=0)
    ub = u.astype(BF16)
    half_ch = S5_WIDTH // 2
    tiles_per_half = STATE_HALF // COL_TILE
    parts = []
    for kt in range(2):
        bu = jnp.dot(ub[:, kt * half_ch:(kt + 1) * half_ch], bw_ref[kt], preferred_element_type=F32)
        cols = []
        for cl in range(tiles_per_half):
            c = kt * tiles_per_half + cl
            ar = jnp.broadcast_to(ar_ref[:, c * LANES:(c + 1) * LANES], (SUBLANES, LANES))
            ai = jnp.broadcast_to(ai_ref[:, c * LANES:(c + 1) * LANES], (SUBLANES, LANES))
            xr = st_ref[:, c * COL_TILE:c * COL_TILE + LANES]
            xi = st_ref[:, c * COL_TILE + LANES:(c + 1) * COL_TILE]
            res_r, res_i = [], []
            for t in range(steps):
                rs = slice(t * SUBLANES, (t + 1) * SUBLANES)
                bur = bu[rs, cl * COL_TILE:cl * COL_TILE + LANES]
                bui = bu[rs, cl * COL_TILE + LANES:(cl + 1) * COL_TILE]
                xr, xi = ar * xr - ai * xi + bur, ar * xi + ai * xr + bui
                res_r.append(xr)
                res_i.append(xi)
            st_ref[:, c * COL_TILE:c * COL_TILE + LANES] = xr
            st_ref[:, c * COL_TILE + LANES:(c + 1) * COL_TILE] = xi
            cols += [jnp.concatenate(res_r, axis=0), jnp.concatenate(res_i, axis=0)]
        xh = jnp.concatenate(cols, axis=1).astype(BF16)
        parts.append(jnp.dot(xh, cw_ref[kt], preferred_element_type=F32))
    y = jnp.concatenate(parts, axis=1)
    y = y + d_ref[...] * u
    y = jax.nn.gelu(y)
    g = jnp.dot(y.astype(BF16), wglu_ref[...], preferred_element_type=F32)
    y = (y * jax.nn.sigmoid(g)).astype(BF16)
    for n, t0 in enumerate(range(0, steps, S5_PERM_STEPS)):
        y_bt = jnp.dot(to_bt, y[n * prow:(n + 1) * prow], preferred_element_type=F32).astype(BF16)
        y_ref[:, t0:t0 + S5_PERM_STEPS, :] = y_bt.reshape(SUBLANES, S5_PERM_STEPS, S5_WIDTH)


def _s5_params(lam_re, lam_im, log_dt, b_re, b_im, c_re, c_im):
    lr = lam_re.astype(F32)
    li = lam_im.astype(F32)
    dt = jnp.exp(log_dt.astype(F32))[:, None]
    mag = jnp.exp(lr * dt)
    abar_r = mag * jnp.cos(li * dt)
    abar_i = mag * jnp.sin(li * dt)
    den = lr * lr + li * li
    nr = abar_r - 1.0
    ni = abar_i
    fr = (nr * lr + ni * li) / den
    fi = (ni * lr - nr * li) / den
    br = b_re.astype(F32)
    bi = b_im.astype(F32)
    bbar_r = fr[..., None] * br - fi[..., None] * bi
    bbar_i = fr[..., None] * bi + fi[..., None] * br
    half_ch = S5_WIDTH // 2
    rp = 2 * S5_STATE
    a_b = jnp.stack([bbar_r, bbar_i], axis=1).transpose(0, 3, 1, 2).reshape(2, half_ch, rp)
    a_c = (jnp.stack([c_re.astype(F32), -c_im.astype(F32)], axis=1).transpose(1, 3, 0, 2)
           .reshape(rp, 2, half_ch).transpose(1, 0, 2))
    lane = jnp.arange(STATE_HALF)
    lane_rp = ((lane // LANES) % 2) * S5_STATE + lane % S5_STATE
    lane_group = 2 * (lane // COL_TILE) + (lane // S5_STATE) % 2
    sel = (jnp.arange(rp)[:, None] == lane_rp[None, :]).astype(F32)
    mask = (jnp.arange(half_ch) // S5_GROUP)[:, None] == lane_group[None, :]
    exact = lax.Precision.HIGHEST
    bw = jnp.where(mask, jnp.einsum('krc,cl->krl', a_b, sel, precision=exact), 0.0).astype(BF16)
    cw = jnp.where(mask.T, jnp.einsum('lc,kcr->klr', sel.T, a_c, precision=exact), 0.0).astype(BF16)
    a_r = abar_r.reshape(1, S5_GROUPS * S5_STATE)
    a_i = abar_i.reshape(1, S5_GROUPS * S5_STATE)
    return bw, cw, a_r, a_i


def _s5(u, lam_re, lam_im, log_dt, b_re, b_im, c_re, c_im, d, w_glu, steps=128):
    bsz, seqlen, _ = u.shape
    assert bsz == SUBLANES
    blk = pl.BlockSpec((bsz, steps, S5_WIDTH), lambda c: (0, c, 0))
    bw, cw, a_r, a_i = _s5_params(lam_re, lam_im, log_dt, b_re, b_im, c_re, c_im)
    return pl.pallas_call(
        functools.partial(_s5_kernel, steps=steps),
        grid=(seqlen // steps,),
        in_specs=[
            blk,
            _const_spec(bw.shape), _const_spec(cw.shape),
            _const_spec(a_r.shape), _const_spec(a_i.shape),
            _const_spec((1, S5_WIDTH)), _const_spec((S5_WIDTH, S5_WIDTH)),
        ],
        out_specs=blk,
        out_shape=jax.ShapeDtypeStruct(u.shape, BF16),
        scratch_shapes=[pltpu.VMEM((SUBLANES, STATE_LANES), F32)],
        compiler_params=pltpu.CompilerParams(
            dimension_semantics=("arbitrary",), vmem_limit_bytes=VMEM_LIMIT),
        name="s5",
    )(u, bw, cw, a_r, a_i, d.astype(F32)[None, :], w_glu.astype(BF16))


ATTN_GROUP_LANES = 256


def _attn_kernel(q_ref, k_ref, v_ref, o_ref, *, blk):
    i = pl.program_id(1)
    n_groups = q_ref.shape[-1] // ATTN_GROUP_LANES
    heads = ATTN_GROUP_LANES // SB_HEAD_DIM
    head_of_lane = lax.broadcasted_iota(jnp.int32, (blk, ATTN_GROUP_LANES), 1) // SB_HEAD_DIM
    row = lax.broadcasted_iota(jnp.int32, (blk, blk), 0)
    col = lax.broadcasted_iota(jnp.int32, (blk, blk), 1)
    later = (row > col).astype(BF16)
    causal = col < row

    def lanes(g):
        return slice(g * ATTN_GROUP_LANES, (g + 1) * ATTN_GROUP_LANES)

    def load_kv(j, g):
        k0 = pl.multiple_of(j * blk, blk)
        return k_ref[0, pl.ds(k0, blk), lanes(g)], v_ref[0, pl.ds(k0, blk), lanes(g)]

    q_rows = []
    for g in range(n_groups):
        qg = q_ref[0, :, lanes(g)]
        q_rows.append(jnp.concatenate(
            [jnp.where(head_of_lane == h, qg, jnp.zeros_like(qg)) for h in range(heads)], axis=0))
    causal_rows = jnp.concatenate([causal] * heads, axis=0)

    def logits(q_all, kb, diag):
        z = lax.dot_general(q_all, kb, (((1,), (1,)), ((), ())), preferred_element_type=F32)
        sp = jnp.log2(1.0 + jnp.exp2(-jnp.abs(z)))
        log_beta = jnp.minimum(z, 0.0) - sp
        log_1m = log_beta - z
        if diag:
            log_1m = jnp.where(causal_rows, log_1m, 0.0)
        return log_beta, log_1m

    def weighted_values(log_beta, log_1m, tail, vb, run, diag):
        expo = log_beta + tail
        if run is not None:
            expo = expo + run
        w = jnp.exp2(expo)
        if diag:
            w = jnp.where(causal_rows, w, 0.0)
        pv = jnp.dot(w.astype(BF16), vb, preferred_element_type=F32)
        out = pv[(heads - 1) * blk:]
        for h in range(heads - 2, -1, -1):
            out = jnp.where(head_of_lane == h, pv[h * blk:(h + 1) * blk], out)
        return out, tail[:, :1] + log_1m[:, :1]

    def group_tile(q_all, kb, vb, run, diag):
        log_beta, log_1m = logits(q_all, kb, diag)
        tail = jnp.dot(log_1m.astype(BF16), later, preferred_element_type=F32)
        return weighted_values(log_beta, log_1m, tail, vb, run, diag)

    has_prev = (i > 0).astype(F32)
    rows = heads * blk
    parts = []
    for g in range(n_groups):
        kb0, vb0 = load_kv(i, g)
        kb1, vb1 = load_kv(jnp.maximum(i - 1, 0), g)
        parts += [(logits(q_rows[g], kb0, True), vb0), (logits(q_rows[g], kb1, False), vb1)]
    tails = jnp.dot(jnp.concatenate([lg[1] for lg, _ in parts], axis=0).astype(BF16), later,
                    preferred_element_type=F32)
    accs, runs = [], []
    for g in range(n_groups):
        (lb0, l10), vb0 = parts[2 * g]
        (lb1, l11), vb1 = parts[2 * g + 1]
        pv0, tot0 = weighted_values(lb0, l10, tails[(2 * g) * rows:(2 * g + 1) * rows], vb0, None, True)
        pv1, tot1 = weighted_values(lb1, l11, tails[(2 * g + 1) * rows:(2 * g + 2) * rows], vb1, tot0, False)
        accs.append(pv0 + has_prev * pv1)
        runs.append(tot0 + has_prev * tot1)

    def live(runs):
        top = runs[0]
        for r in runs[1:]:
            top = jnp.maximum(top, r)
        return jnp.max(top) > ZERO_WEIGHT_LOG2

    def cond(state):
        jj, alive = state[0], state[1]
        return jnp.logical_and(jj <= i, alive)

    def body(state):
        jj, accs, runs = state[0], state[2], state[3]
        new_accs, new_runs = [], []
        for g in range(n_groups):
            kb, vb = load_kv(i - jj, g)
            pv, tot = group_tile(q_rows[g], kb, vb, runs[g], False)
            new_accs.append(accs[g] + pv)
            new_runs.append(runs[g] + tot)
        return (jj + 1, live(new_runs), new_accs, new_runs)

    state = lax.while_loop(cond, body, (jnp.int32(2), live(runs), accs, runs))
    for g in range(n_groups):
        o_ref[0, :, lanes(g)] = state[2][g].astype(BF16)


def _sbattn(q, k, v, blk=256):
    bsz, seqlen, width = q.shape
    qspec = pl.BlockSpec((1, blk, width), lambda b, i: (b, i, 0))
    kvspec = pl.BlockSpec((1, seqlen, width), lambda b, i: (b, 0, 0))
    return pl.pallas_call(
        functools.partial(_attn_kernel, blk=blk),
        grid=(bsz, seqlen // blk),
        in_specs=[qspec, kvspec, kvspec],
        out_specs=qspec,
        out_shape=jax.ShapeDtypeStruct((bsz, seqlen, width), BF16),
        compiler_params=pltpu.CompilerParams(
            dimension_semantics=("parallel", "arbitrary"), vmem_limit_bytes=VMEM_LIMIT),
        name="sbattn",
    )(q, k, v)


FF_CHUNK = 1024


def _mlp_ple(h, p_ref, lnm_ref, wup_ref, wdn_ref, lnp_ref, wg_ref, wpu_ref, o_ref):
    hn = _rms(h, lnm_ref[...]).astype(BF16)
    acc = h
    for c in range(0, D_FF, FF_CHUNK):
        a = jnp.dot(hn, wup_ref[:, c:c + FF_CHUNK], preferred_element_type=F32)
        a = jnp.square(jnp.maximum(a, 0.0)).astype(BF16)
        acc = acc + jnp.dot(a, wdn_ref[c:c + FF_CHUNK, :], preferred_element_type=F32)
    gate = jax.nn.sigmoid(jnp.dot(_rms(acc, lnp_ref[...]).astype(BF16), wg_ref[...], preferred_element_type=F32))
    pe = jnp.dot(p_ref[0].astype(BF16), wpu_ref[...], preferred_element_type=F32)
    o_ref[0] = acc + pe * gate


def _layer_even_kernel(x_ref, s5_ref, sb_ref, wout_ref, p_ref, lnm_ref, wup_ref, wdn_ref, lnp_ref, wg_ref,
                       wpu_ref, o_ref):
    mixed = (jnp.dot(s5_ref[0], wout_ref[:S5_WIDTH, :], preferred_element_type=F32)
             + jnp.dot(sb_ref[0], wout_ref[S5_WIDTH:, :], preferred_element_type=F32))
    _mlp_ple(x_ref[0] + mixed, p_ref, lnm_ref, wup_ref, wdn_ref, lnp_ref, wg_ref, wpu_ref, o_ref)


def _layer_odd_kernel(h_ref, halo_ref, lno_ref, pw_ref, ps_ref, p_ref, lnm_ref, wup_ref, wdn_ref, lnp_ref,
                      wg_ref, wpu_ref, o_ref, *, tm):
    i = pl.program_id(1)
    h = h_ref[0]
    hn = _rms(h, lno_ref[...])
    halo = _rms(halo_ref[0], lno_ref[...]) * (i > 0).astype(F32)
    ext = jnp.concatenate([halo, hn], axis=0)
    t = lax.broadcasted_iota(jnp.int32, (tm, 1), 0) + i * tm
    outs = []
    for g, window in enumerate(POOL_WINDOWS):
        sl = slice(g * POOL_GROUP, (g + 1) * POOL_GROUP)
        s = ext[:, sl]
        span = 1
        while span < window:
            s = s + jnp.concatenate([jnp.zeros((span, POOL_GROUP), F32), s[:-span]], axis=0)
            span *= 2
        count = jnp.minimum(t + 1, window).astype(F32)
        y = s[POOL_HALO:] / count - hn[:, sl]
        outs.append(jnp.dot(y.astype(BF16), pw_ref[g], preferred_element_type=F32))
    mixed = jnp.concatenate(outs, axis=1) * ps_ref[...]
    _mlp_ple(h + mixed, p_ref, lnm_ref, wup_ref, wdn_ref, lnp_ref, wg_ref, wpu_ref, o_ref)


def _tail_specs(tm, layer):
    return [
        pl.BlockSpec((None, 1, tm, PLE_DIM), lambda b, i: (layer, b, i, 0)),
        _const_spec((1, D_MODEL)), _const_spec((D_MODEL, D_FF)), _const_spec((D_FF, D_MODEL)),
        _const_spec((1, D_MODEL)), _const_spec((D_MODEL, D_MODEL)), _const_spec((PLE_DIM, D_MODEL)),
    ]


def _tail_args(p, ln_mlp, w_up, w_down, ln_ple, w_gate, w_ple_up):
    return (p, ln_mlp.astype(F32)[None, :], w_up.astype(BF16), w_down.astype(BF16),
            ln_ple.astype(F32)[None, :], w_gate.astype(BF16), w_ple_up.astype(BF16))


def _layer_even(x, s5, sb, w_out, tail, layer, tm=512):
    bsz, seqlen, _ = x.shape
    row = lambda w: pl.BlockSpec((1, tm, w), lambda b, i: (b, i, 0))
    return pl.pallas_call(
        _layer_even_kernel,
        grid=(bsz, seqlen // tm),
        in_specs=[row(D_MODEL),
                  row(S5_WIDTH),
                  row(SB_WIDTH),
                  _const_spec((D_MODEL, D_MODEL))] + _tail_specs(tm, layer),
        out_specs=row(D_MODEL),
        out_shape=jax.ShapeDtypeStruct(x.shape, F32),
        compiler_params=pltpu.CompilerParams(
            dimension_semantics=("parallel", "parallel"), vmem_limit_bytes=VMEM_LIMIT),
        name="layer_even",
    )(x, s5, sb, w_out.astype(BF16), *tail)


def _layer_odd(h, ln_odd, pool_w, pool_scale, tail, layer, tm=512):
    bsz, seqlen, _ = h.shape
    row = lambda w: pl.BlockSpec((1, tm, w), lambda b, i: (b, i, 0))
    halo_blocks = tm // POOL_HALO
    return pl.pallas_call(
        functools.partial(_layer_odd_kernel, tm=tm),
        grid=(bsz, seqlen // tm),
        in_specs=[row(D_MODEL),
                  pl.BlockSpec((1, POOL_HALO, D_MODEL), lambda b, i: (b, jnp.maximum(i * halo_blocks - 1, 0), 0)),
                  _const_spec((1, D_MODEL)),
                  _const_spec((len(POOL_WINDOWS), POOL_GROUP, POOL_GROUP)),
                  _const_spec((1, D_MODEL))] + _tail_specs(tm, layer),
        out_specs=row(D_MODEL),
        out_shape=jax.ShapeDtypeStruct(h.shape, F32),
        compiler_params=pltpu.CompilerParams(
            dimension_semantics=("parallel", "parallel"), vmem_limit_bytes=VMEM_LIMIT),
        name="layer_odd",
    )(h, h, ln_odd.astype(F32)[None, :], pool_w.astype(BF16), pool_scale.astype(F32)[None, :], *tail)


def kernel(x, p, ln_mix_even, w_in_even, s5_lambda_re, s5_lambda_im, s5_log_dt, s5_b_re, s5_b_im, s5_c_re,
           s5_c_im, s5_d, s5_w_glu, sb_q_gain, sb_k_gain, w_out_even, ln_mix_odd, pool_w, pool_scale, ln_mlp,
           w_mlp_up, w_mlp_down, ln_ple, w_ple_gate, w_ple_up):
    h = x
    for i in range(p.shape[0]):
        j = i // 2
        tail = _tail_args(p, ln_mlp[i], w_mlp_up[i], w_mlp_down[i], ln_ple[i], w_ple_gate[i], w_ple_up[i])
        if i % 2 == 0:
            u, q, k, v = _inproj(h, ln_mix_even[j], w_in_even[j], sb_q_gain[j], sb_k_gain[j])
            s5 = _s5(u, s5_lambda_re[j], s5_lambda_im[j], s5_log_dt[j], s5_b_re[j], s5_b_im[j],
                     s5_c_re[j], s5_c_im[j], s5_d[j], s5_w_glu[j])
            sb = _sbattn(q, k, v)
            h = _layer_even(h, s5, sb, w_out_even[j], tail, i)
        else:
            h = _layer_odd(h, ln_mix_odd[j], pool_w[j], pool_scale[j], tail, i)
    return h
```

```python
import functools
import math

import jax
import jax.numpy as jnp
from jax import lax
from jax.experimental import pallas as pl
from jax.experimental.pallas import tpu as pltpu

F32 = jnp.float32
BF16 = jnp.bfloat16

D_MODEL = 1024
S5_WIDTH = 512
S5_GROUP = 16
S5_GROUPS = 32
S5_STATE = 64
SB_HEAD_DIM = 64
SB_WIDTH = 512
IN_WIDTH = S5_WIDTH + 3 * SB_WIDTH
POOL_WINDOWS = (2, 4, 8, 16)
POOL_GROUP = 256
POOL_HALO = max(POOL_WINDOWS)
D_FF = 4 * D_MODEL
PLE_DIM = 256
EPS = 1e-6
LOG2_E = math.log2(math.e)
ZERO_WEIGHT_LOG2 = -150.0

LANES = 128
SUBLANES = 8
STATE_LANES = 2 * S5_GROUPS * S5_STATE
STATE_HALF = STATE_LANES // 2
COL_TILE = 2 * LANES

V7X_VMEM_BYTES = 64 * 1024 * 1024
VMEM_LIMIT = V7X_VMEM_BYTES * 7 // 8


def _const_spec(shape):
    nd = len(shape)
    return pl.BlockSpec(shape, lambda *_: (0,) * nd, pipeline_mode=pl.Buffered(1))


def _rms(x, gain):
    ms = jnp.mean(x * x, axis=-1, keepdims=True)
    return x * lax.rsqrt(ms + EPS) * gain


def _inproj_kernel(x_ref, ln_ref, w_ref, qg_ref, kg_ref, hm_ref, u_ref, q_ref, k_ref, v_ref):
    hn = _rms(x_ref[0], ln_ref[...]).astype(BF16)
    proj = jnp.dot(hn, w_ref[...], preferred_element_type=F32)
    u_ref[0] = proj[:, :S5_WIDTH].astype(BF16)
    hm = hm_ref[...]

    def head_norm(t, gain):
        ms = jnp.dot((t * t).astype(BF16), hm, preferred_element_type=F32)
        return t * lax.rsqrt(ms + EPS) * gain

    q = proj[:, S5_WIDTH:S5_WIDTH + SB_WIDTH]
    k = proj[:, S5_WIDTH + SB_WIDTH:S5_WIDTH + 2 * SB_WIDTH]
    q_ref[0] = (head_norm(q, qg_ref[...]) * (SB_HEAD_DIM ** -0.5 * LOG2_E)).astype(BF16)
    k_ref[0] = head_norm(k, kg_ref[...]).astype(BF16)
    v_ref[0] = proj[:, S5_WIDTH + 2 * SB_WIDTH:].astype(BF16)


def _inproj(x, ln, w_in, q_gain, k_gain, tl=1024):
    bsz, seqlen, _ = x.shape
    heads = SB_WIDTH // SB_HEAD_DIM
    hm = jnp.kron(jnp.eye(heads, dtype=F32), jnp.full((SB_HEAD_DIM, SB_HEAD_DIM), 1.0 / SB_HEAD_DIM, F32)).astype(BF16)
    qg = jnp.tile(q_gain.astype(F32), heads)[None, :]
    kg = jnp.tile(k_gain.astype(F32), heads)[None, :]
    act = lambda: pl.BlockSpec((1, tl, SB_WIDTH), lambda b, i: (b, i, 0))
    return pl.pallas_call(
        _inproj_kernel,
        grid=(bsz, seqlen // tl),
        in_specs=[
            pl.BlockSpec((1, tl, D_MODEL), lambda b, i: (b, i, 0)),
            _const_spec((1, D_MODEL)),
            _const_spec((D_MODEL, IN_WIDTH)),
            _const_spec((1, SB_WIDTH)),
            _const_spec((1, SB_WIDTH)),
            _const_spec((SB_WIDTH, SB_WIDTH)),
        ],
        out_specs=[act(), act(), act(), act()],
        out_shape=[
            jax.ShapeDtypeStruct((bsz, seqlen, S5_WIDTH), BF16),
            jax.ShapeDtypeStruct((bsz, seqlen, SB_WIDTH), BF16),
            jax.ShapeDtypeStruct((bsz, seqlen, SB_WIDTH), BF16),
            jax.ShapeDtypeStruct((bsz, seqlen, SB_WIDTH), BF16),
        ],
        compiler_params=pltpu.CompilerParams(
            dimension_semantics=("parallel", "parallel"), vmem_limit_bytes=VMEM_LIMIT),
        name="inproj",
    )(x, ln.astype(F32)[None, :], w_in.astype(BF16), qg, kg, hm)


S5_PERM_STEPS = 32


def _s5_kernel(u_ref, bw_ref, cw_ref, ar_ref, ai_ref, d_ref, wglu_ref, y_ref, st_ref, *, steps):
    @pl.when(pl.program_id(0) == 0)
    def _():
        st_ref[...] = jnp.zeros_like(st_ref)

    prow = S5_PERM_STEPS * SUBLANES
    r = lax.broadcasted_iota(jnp.int32, (prow, prow), 0)
    c = lax.broadcasted_iota(jnp.int32, (prow, prow), 1)
    to_tb = jnp.logical_and(r // SUBLANES == c % S5_PERM_STEPS, r % SUBLANES == c // S5_PERM_STEPS).astype(BF16)
    to_bt = jnp.logical_and(c // SUBLANES == r % S5_PERM_STEPS, c % SUBLANES == r // S5_PERM_STEPS).astype(BF16)
    u = jnp.concatenate(
        [jnp.dot(to_tb, u_ref[:, t0:t0 + S5_PERM_STEPS, :].reshape(prow, S5_WIDTH), preferred_element_type=F32)
         for t0 in range(0, steps, S5_PERM_STEPS)], axis=0)
    ub = u.astype(BF16)
    half_ch = S5_WIDTH // 2
    tiles_per_half = STATE_HALF // COL_TILE
    parts = []
    for kt in range(2):
        bu = jnp.dot(ub[:, kt * half_ch:(kt + 1) * half_ch], bw_ref[kt], preferred_element_type=F32)
        cols = []
        for cl in range(tiles_per_half):
            c = kt * tiles_per_half + cl
            ar = jnp.broadcast_to(ar_ref[:, c * LANES:(c + 1) * LANES], (SUBLANES, LANES))
            ai = jnp.broadcast_to(ai_ref[:, c * LANES:(c + 1) * LANES], (SUBLANES, LANES))
            xr = st_ref[:, c * COL_TILE:c * COL_TILE + LANES]
            xi = st_ref[:, c * COL_TILE + LANES:(c + 1) * COL_TILE]
            res_r, res_i = [], []
            for t in range(steps):
                rs = slice(t * SUBLANES, (t + 1) * SUBLANES)
                bur = bu[rs, cl * COL_TILE:cl * COL_TILE + LANES]
                bui = bu[rs, cl * COL_TILE + LANES:(cl + 1) * COL_TILE]
                xr, xi = ar * xr - ai * xi + bur, ar * xi + ai * xr + bui
                res_r.append(xr)
                res_i.append(xi)
            st_ref[:, c * COL_TILE:c * COL_TILE + LANES] = xr
            st_ref[:, c * COL_TILE + LANES:(c + 1) * COL_TILE] = xi
            cols += [jnp.concatenate(res_r, axis=0), jnp.concatenate(res_i, axis=0)]
        xh = jnp.concatenate(cols, axis=1).astype(BF16)
        parts.append(jnp.dot(xh, cw_ref[kt], preferred_element_type=F32))
    y = jnp.concatenate(parts, axis=1)
    y = y + d_ref[...] * u
    y = jax.nn.gelu(y)
    g = jnp.dot(y.astype(BF16), wglu_ref[...], preferred_element_type=F32)
    y = (y * jax.nn.sigmoid(g)).astype(BF16)
    for n, t0 in enumerate(range(0, steps, S5_PERM_STEPS)):
        y_bt = jnp.dot(to_bt, y[n * prow:(n + 1) * prow], preferred_element_type=F32).astype(BF16)
        y_ref[:, t0:t0 + S5_PERM_STEPS, :] = y_bt.reshape(SUBLANES, S5_PERM_STEPS, S5_WIDTH)


def _s5_params(lam_re, lam_im, log_dt, b_re, b_im, c_re, c_im):
    lr = lam_re.astype(F32)
    li = lam_im.astype(F32)
    dt = jnp.exp(log_dt.astype(F32))[:, None]
    mag = jnp.exp(lr * dt)
    abar_r = mag * jnp.cos(li * dt)
    abar_i = mag * jnp.sin(li * dt)
    den = lr * lr + li * li
    nr = abar_r - 1.0
    ni = abar_i
    fr = (nr * lr + ni * li) / den
    fi = (ni * lr - nr * li) / den
    br = b_re.astype(F32)
    bi = b_im.astype(F32)
    bbar_r = fr[..., None] * br - fi[..., None] * bi
    bbar_i = fr[..., None] * bi + fi[..., None] * br
    half_ch = S5_WIDTH // 2
    rp = 2 * S5_STATE
    a_b = jnp.stack([bbar_r, bbar_i], axis=1).transpose(0, 3, 1, 2).reshape(2, half_ch, rp)
    a_c = (jnp.stack([c_re.astype(F32), -c_im.astype(F32)], axis=1).transpose(1, 3, 0, 2)
           .reshape(rp, 2, half_ch).transpose(1, 0, 2))
    lane = jnp.arange(STATE_HALF)
    lane_rp = ((lane // LANES) % 2) * S5_STATE + lane % S5_STATE
    lane_group = 2 * (lane // COL_TILE) + (lane // S5_STATE) % 2
    sel = (jnp.arange(rp)[:, None] == lane_rp[None, :]).astype(F32)
    mask = (jnp.arange(half_ch) // S5_GROUP)[:, None] == lane_group[None, :]
    exact = lax.Precision.HIGHEST
    bw = jnp.where(mask, jnp.einsum('krc,cl->krl', a_b, sel, precision=exact), 0.0).astype(BF16)
    cw = jnp.where(mask.T, jnp.einsum('lc,kcr->klr', sel.T, a_c, precision=exact), 0.0).astype(BF16)
    a_r = abar_r.reshape(1, S5_GROUPS * S5_STATE)
    a_i = abar_i.reshape(1, S5_GROUPS * S5_STATE)
    return bw, cw, a_r, a_i


def _s5(u, lam_re, lam_im, log_dt, b_re, b_im, c_re, c_im, d, w_glu, steps=128):
    bsz, seqlen, _ = u.shape
    assert bsz == SUBLANES
    blk = pl.BlockSpec((bsz, steps, S5_WIDTH), lambda c: (0, c, 0))
    bw, cw, a_r, a_i = _s5_params(lam_re, lam_im, log_dt, b_re, b_im, c_re, c_im)
    return pl.pallas_call(
        functools.partial(_s5_kernel, steps=steps),
        grid=(seqlen // steps,),
        in_specs=[
            blk,
            _const_spec(bw.shape), _const_spec(cw.shape),
            _const_spec(a_r.shape), _const_spec(a_i.shape),
            _const_spec((1, S5_WIDTH)), _const_spec((S5_WIDTH, S5_WIDTH)),
        ],
        out_specs=blk,
        out_shape=jax.ShapeDtypeStruct(u.shape, BF16),
        scratch_shapes=[pltpu.VMEM((SUBLANES, STATE_LANES), F32)],
        compiler_params=pltpu.CompilerParams(
            dimension_semantics=("arbitrary",), vmem_limit_bytes=VMEM_LIMIT),
        name="s5",
    )(u, bw, cw, a_r, a_i, d.astype(F32)[None, :], w_glu.astype(BF16))


ATTN_GROUP_LANES = 256


def _attn_kernel(q_ref, k_ref, v_ref, o_ref, *, blk):
    i = pl.program_id(1)
    n_groups = q_ref.shape[-1] // ATTN_GROUP_LANES
    heads = ATTN_GROUP_LANES // SB_HEAD_DIM
    head_of_lane = lax.broadcasted_iota(jnp.int32, (blk, ATTN_GROUP_LANES), 1) // SB_HEAD_DIM
    row = lax.broadcasted_iota(jnp.int32, (blk, blk), 0)
    col = lax.broadcasted_iota(jnp.int32, (blk, blk), 1)
    from_key = (row >= col).astype(BF16)
    causal = col < row

    def lanes(g):
        return slice(g * ATTN_GROUP_LANES, (g + 1) * ATTN_GROUP_LANES)

    def load_kv(j, g):
        k0 = pl.multiple_of(j * blk, blk)
        return k_ref[0, pl.ds(k0, blk), lanes(g)], v_ref[0, pl.ds(k0, blk), lanes(g)]

    q_rows = []
    for g in range(n_groups):
        qg = q_ref[0, :, lanes(g)]
        q_rows.append(jnp.concatenate(
            [jnp.where(head_of_lane == h, qg, jnp.zeros_like(qg)) for h in range(heads)], axis=0))
    causal_rows = jnp.concatenate([causal] * heads, axis=0)

    def logits(q_all, kb, diag):
        z = lax.dot_general(q_all, kb, (((1,), (1,)), ((), ())), preferred_element_type=F32)
        neg_part = jnp.minimum(z, 0.0)
        neg_relu = neg_part - z
        log_1m = neg_relu - jnp.log2(1.0 + jnp.exp2(neg_part + neg_relu))
        if diag:
            log_1m = jnp.where(causal_rows, log_1m, 0.0)
        return z, log_1m

    def weighted_values(z, tail, vb, run, diag):
        expo = z + tail
        if run is not None:
            expo = expo + run
        w = jnp.exp2(expo)
        if diag:
            w = jnp.where(causal_rows, w, 0.0)
        pv = jnp.dot(w.astype(BF16), vb, preferred_element_type=F32)
        out = pv[(heads - 1) * blk:]
        for h in range(heads - 2, -1, -1):
            out = jnp.where(head_of_lane == h, pv[h * blk:(h + 1) * blk], out)
        return out, tail[:, :1]

    def group_tile(q_all, kb, vb, run, diag):
        z, log_1m = logits(q_all, kb, diag)
        tail = jnp.dot(log_1m.astype(BF16), from_key, preferred_element_type=F32)
        return weighted_values(z, tail, vb, run, diag)

    has_prev = (i > 0).astype(F32)
    rows = heads * blk
    parts = []
    for g in range(n_groups):
        kb0, vb0 = load_kv(i, g)
        kb1, vb1 = load_kv(jnp.maximum(i - 1, 0), g)
        parts += [(logits(q_rows[g], kb0, True), vb0), (logits(q_rows[g], kb1, False), vb1)]
    tails = jnp.dot(jnp.concatenate([lg[1] for lg, _ in parts], axis=0).astype(BF16), from_key,
                    preferred_element_type=F32)
    accs, runs = [], []
    for g in range(n_groups):
        (z0, _), vb0 = parts[2 * g]
        (z1, _), vb1 = parts[2 * g + 1]
        pv0, tot0 = weighted_values(z0, tails[(2 * g) * rows:(2 * g + 1) * rows], vb0, None, True)
        pv1, tot1 = weighted_values(z1, tails[(2 * g + 1) * rows:(2 * g + 2) * rows], vb1, tot0, False)
        accs.append(pv0 + has_prev * pv1)
        runs.append(tot0 + has_prev * tot1)

    def live(runs):
        top = runs[0]
        for r in runs[1:]:
            top = jnp.maximum(top, r)
        return jnp.max(top) > ZERO_WEIGHT_LOG2

    def cond(state):
        jj, alive = state[0], state[1]
        return jnp.logical_and(jj <= i, alive)

    def body(state):
        jj, accs, runs = state[0], state[2], state[3]
        new_accs, new_runs = [], []
        for g in range(n_groups):
            kb, vb = load_kv(i - jj, g)
            pv, tot = group_tile(q_rows[g], kb, vb, runs[g], False)
            new_accs.append(accs[g] + pv)
            new_runs.append(runs[g] + tot)
        return (jj + 1, live(new_runs), new_accs, new_runs)

    state = lax.while_loop(cond, body, (jnp.int32(2), live(runs), accs, runs))
    for g in range(n_groups):
        o_ref[0, :, lanes(g)] = state[2][g].astype(BF16)


def _sbattn(q, k, v, blk=256):
    bsz, seqlen, width = q.shape
    qspec = pl.BlockSpec((1, blk, width), lambda b, i: (b, i, 0))
    kvspec = pl.BlockSpec((1, seqlen, width), lambda b, i: (b, 0, 0))
    return pl.pallas_call(
        functools.partial(_attn_kernel, blk=blk),
        grid=(bsz, seqlen // blk),
        in_specs=[qspec, kvspec, kvspec],
        out_specs=qspec,
        out_shape=jax.ShapeDtypeStruct((bsz, seqlen, width), BF16),
        compiler_params=pltpu.CompilerParams(
            dimension_semantics=("parallel", "arbitrary"), vmem_limit_bytes=VMEM_LIMIT),
        name="sbattn",
    )(q, k, v)


FF_CHUNK = 1024


def _mlp_ple(h, p_ref, lnm_ref, wup_ref, wdn_ref, lnp_ref, wg_ref, wpu_ref, o_ref):
    hn = _rms(h, lnm_ref[...]).astype(BF16)
    acc = h
    for c in range(0, D_FF, FF_CHUNK):
        a = jnp.dot(hn, wup_ref[:, c:c + FF_CHUNK], preferred_element_type=F32)
        a = jnp.square(jnp.maximum(a, 0.0)).astype(BF16)
        acc = acc + jnp.dot(a, wdn_ref[c:c + FF_CHUNK, :], preferred_element_type=F32)
    gate = jax.nn.sigmoid(jnp.dot(_rms(acc, lnp_ref[...]).astype(BF16), wg_ref[...], preferred_element_type=F32))
    pe = jnp.dot(p_ref[0].astype(BF16), wpu_ref[...], preferred_element_type=F32)
    o_ref[0] = acc + pe * gate


def _layer_even_kernel(x_ref, s5_ref, sb_ref, wout_ref, p_ref, lnm_ref, wup_ref, wdn_ref, lnp_ref, wg_ref,
                       wpu_ref, o_ref):
    mixed = (jnp.dot(s5_ref[0], wout_ref[:S5_WIDTH, :], preferred_element_type=F32)
             + jnp.dot(sb_ref[0], wout_ref[S5_WIDTH:, :], preferred_element_type=F32))
    _mlp_ple(x_ref[0] + mixed, p_ref, lnm_ref, wup_ref, wdn_ref, lnp_ref, wg_ref, wpu_ref, o_ref)


def _layer_odd_kernel(h_ref, halo_ref, lno_ref, pw_ref, ps_ref, p_ref, lnm_ref, wup_ref, wdn_ref, lnp_ref,
                      wg_ref, wpu_ref, o_ref, *, tm):
    i = pl.program_id(1)
    h = h_ref[0]
    hn = _rms(h, lno_ref[...])
    halo = _rms(halo_ref[0], lno_ref[...]) * (i > 0).astype(F32)
    ext = jnp.concatenate([halo, hn], axis=0)
    t = lax.broadcasted_iota(jnp.int32, (tm, 1), 0) + i * tm
    outs = []
    for g, window in enumerate(POOL_WINDOWS):
        sl = slice(g * POOL_GROUP, (g + 1) * POOL_GROUP)
        s = ext[:, sl]
        span = 1
        while span < window:
            s = s + jnp.concatenate([jnp.zeros((span, POOL_GROUP), F32), s[:-span]], axis=0)
            span *= 2
        count = jnp.minimum(t + 1, window).astype(F32)
        y = s[POOL_HALO:] / count - hn[:, sl]
        outs.append(jnp.dot(y.astype(BF16), pw_ref[g], preferred_element_type=F32))
    mixed = jnp.concatenate(outs, axis=1) * ps_ref[...]
    _mlp_ple(h + mixed, p_ref, lnm_ref, wup_ref, wdn_ref, lnp_ref, wg_ref, wpu_ref, o_ref)


def _tail_specs(tm, layer):
    return [
        pl.BlockSpec((None, 1, tm, PLE_DIM), lambda b, i: (layer, b, i, 0)),
        _const_spec((1, D_MODEL)), _const_spec((D_MODEL, D_FF)), _const_spec((D_FF, D_MODEL)),
        _const_spec((1, D_MODEL)), _const_spec((D_MODEL, D_MODEL)), _const_spec((PLE_DIM, D_MODEL)),
    ]


def _tail_args(p, ln_mlp, w_up, w_down, ln_ple, w_gate, w_ple_up):
    return (p, ln_mlp.astype(F32)[None, :], w_up.astype(BF16), w_down.astype(BF16),
            ln_ple.astype(F32)[None, :], w_gate.astype(BF16), w_ple_up.astype(BF16))


def _layer_even(x, s5, sb, w_out, tail, layer, tm=512):
    bsz, seqlen, _ = x.shape
    row = lambda w: pl.BlockSpec((1, tm, w), lambda b, i: (b, i, 0))
    return pl.pallas_call(
        _layer_even_kernel,
        grid=(bsz, seqlen // tm),
        in_specs=[row(D_MODEL),
                  row(S5_WIDTH),
                  row(SB_WIDTH),
                  _const_spec((D_MODEL, D_MODEL))] + _tail_specs(tm, layer),
        out_specs=row(D_MODEL),
        out_shape=jax.ShapeDtypeStruct(x.shape, F32),
        compiler_params=pltpu.CompilerParams(
            dimension_semantics=("parallel", "parallel"), vmem_limit_bytes=VMEM_LIMIT),
        name="layer_even",
    )(x, s5, sb, w_out.astype(BF16), *tail)


def _layer_odd(h, ln_odd, pool_w, pool_scale, tail, layer, tm=512):
    bsz, seqlen, _ = h.shape
    row = lambda w: pl.BlockSpec((1, tm, w), lambda b, i: (b, i, 0))
    halo_blocks = tm // POOL_HALO
    return pl.pallas_call(
        functools.partial(_layer_odd_kernel, tm=tm),
        grid=(bsz, seqlen // tm),
        in_specs=[row(D_MODEL),
                  pl.BlockSpec((1, POOL_HALO, D_MODEL), lambda b, i: (b, jnp.maximum(i * halo_blocks - 1, 0), 0)),
                  _const_spec((1, D_MODEL)),
                  _const_spec((len(POOL_WINDOWS), POOL_GROUP, POOL_GROUP)),
                  _const_spec((1, D_MODEL))] + _tail_specs(tm, layer),
        out_specs=row(D_MODEL),
        out_shape=jax.ShapeDtypeStruct(h.shape, F32),
        compiler_params=pltpu.CompilerParams(
            dimension_semantics=("parallel", "parallel"), vmem_limit_bytes=VMEM_LIMIT),
        name="layer_odd",
    )(h, h, ln_odd.astype(F32)[None, :], pool_w.astype(BF16), pool_scale.astype(F32)[None, :], *tail)


def kernel(x, p, ln_mix_even, w_in_even, s5_lambda_re, s5_lambda_im, s5_log_dt, s5_b_re, s5_b_im, s5_c_re,
           s5_c_im, s5_d, s5_w_glu, sb_q_gain, sb_k_gain, w_out_even, ln_mix_odd, pool_w, pool_scale, ln_mlp,
           w_mlp_up, w_mlp_down, ln_ple, w_ple_gate, w_ple_up):
    h = x
    for i in range(p.shape[0]):
        j = i // 2
        tail = _tail_args(p, ln_mlp[i], w_mlp_up[i], w_mlp_down[i], ln_ple[i], w_ple_gate[i], w_ple_up[i])
        if i % 2 == 0:
            u, q, k, v = _inproj(h, ln_mix_even[j], w_in_even[j], sb_q_gain[j], sb_k_gain[j])
            s5 = _s5(u, s5_lambda_re[j], s5_lambda_im[j], s5_log_dt[j], s5_b_re[j], s5_b_im[j],
                     s5_c_re[j], s5_c_im[j], s5_d[j], s5_w_glu[j])
            sb = _sbattn(q, k, v)
            h = _layer_even(h, s5, sb, w_out_even[j], tail, i)
        else:
            h = _layer_odd(h, ln_mix_odd[j], pool_w[j], pool_scale[j], tail, i)
    return h
```

```python
import functools
import math

import jax
import jax.numpy as jnp
from jax import lax
from jax.experimental import pallas as pl
from jax.experimental.pallas import tpu as pltpu

F32 = jnp.float32
BF16 = jnp.bfloat16

D_MODEL = 1024
S5_WIDTH = 512
S5_GROUP = 16
S5_GROUPS = 32
S5_STATE = 64
SB_HEAD_DIM = 64
SB_WIDTH = 512
IN_WIDTH = S5_WIDTH + 3 * SB_WIDTH
POOL_WINDOWS = (2, 4, 8, 16)
POOL_GROUP = 256
POOL_HALO = max(POOL_WINDOWS)
D_FF = 4 * D_MODEL
PLE_DIM = 256
EPS = 1e-6
LOG2_E = math.log2(math.e)
ZERO_WEIGHT_LOG2 = -150.0

LANES = 128
SUBLANES = 8
STATE_LANES = 2 * S5_GROUPS * S5_STATE
STATE_HALF = STATE_LANES // 2
COL_TILE = 2 * LANES

V7X_VMEM_BYTES = 64 * 1024 * 1024
VMEM_LIMIT = V7X_VMEM_BYTES * 7 // 8


def _const_spec(shape):
    nd = len(shape)
    return pl.BlockSpec(shape, lambda *_: (0,) * nd, pipeline_mode=pl.Buffered(1))


def _rms(x, gain):
    ms = jnp.mean(x * x, axis=-1, keepdims=True)
    return x * lax.rsqrt(ms + EPS) * gain


def _inproj_kernel(x_ref, ln_ref, w_ref, qg_ref, kg_ref, hm_ref, u_ref, q_ref, k_ref, v_ref):
    hn = _rms(x_ref[0], ln_ref[...]).astype(BF16)
    proj = jnp.dot(hn, w_ref[...], preferred_element_type=F32)
    u_ref[0] = proj[:, :S5_WIDTH].astype(BF16)
    hm = hm_ref[...]

    def head_norm(t, gain):
        ms = jnp.dot((t * t).astype(BF16), hm, preferred_element_type=F32)
        return t * lax.rsqrt(ms + EPS) * gain

    q = proj[:, S5_WIDTH:S5_WIDTH + SB_WIDTH]
    k = proj[:, S5_WIDTH + SB_WIDTH:S5_WIDTH + 2 * SB_WIDTH]
    q_ref[0] = (head_norm(q, qg_ref[...]) * (SB_HEAD_DIM ** -0.5 * LOG2_E)).astype(BF16)
    k_ref[0] = head_norm(k, kg_ref[...]).astype(BF16)
    v_ref[0] = proj[:, S5_WIDTH + 2 * SB_WIDTH:].astype(BF16)


def _inproj(x, ln, w_in, q_gain, k_gain, tl=1024):
    bsz, seqlen, _ = x.shape
    heads = SB_WIDTH // SB_HEAD_DIM
    hm = jnp.kron(jnp.eye(heads, dtype=F32), jnp.full((SB_HEAD_DIM, SB_HEAD_DIM), 1.0 / SB_HEAD_DIM, F32)).astype(BF16)
    qg = jnp.tile(q_gain.astype(F32), heads)[None, :]
    kg = jnp.tile(k_gain.astype(F32), heads)[None, :]
    act = lambda: pl.BlockSpec((1, tl, SB_WIDTH), lambda b, i: (b, i, 0))
    return pl.pallas_call(
        _inproj_kernel,
        grid=(bsz, seqlen // tl),
        in_specs=[
            pl.BlockSpec((1, tl, D_MODEL), lambda b, i: (b, i, 0)),
            _const_spec((1, D_MODEL)),
            _const_spec((D_MODEL, IN_WIDTH)),
            _const_spec((1, SB_WIDTH)),
            _const_spec((1, SB_WIDTH)),
            _const_spec((SB_WIDTH, SB_WIDTH)),
        ],
        out_specs=[act(), act(), act(), act()],
        out_shape=[
            jax.ShapeDtypeStruct((bsz, seqlen, S5_WIDTH), BF16),
            jax.ShapeDtypeStruct((bsz, seqlen, SB_WIDTH), BF16),
            jax.ShapeDtypeStruct((bsz, seqlen, SB_WIDTH), BF16),
            jax.ShapeDtypeStruct((bsz, seqlen, SB_WIDTH), BF16),
        ],
        compiler_params=pltpu.CompilerParams(
            dimension_semantics=("parallel", "parallel"), vmem_limit_bytes=VMEM_LIMIT),
        name="inproj",
    )(x, ln.astype(F32)[None, :], w_in.astype(BF16), qg, kg, hm)


S5_PERM_STEPS = 32


def _s5_kernel(u_ref, bw_ref, cw_ref, ar_ref, ai_ref, d_ref, wglu_ref, y_ref, st_ref, *, steps):
    @pl.when(pl.program_id(0) == 0)
    def _():
        st_ref[...] = jnp.zeros_like(st_ref)

    prow = S5_PERM_STEPS * SUBLANES
    r = lax.broadcasted_iota(jnp.int32, (prow, prow), 0)
    c = lax.broadcasted_iota(jnp.int32, (prow, prow), 1)
    to_tb = jnp.logical_and(r // SUBLANES == c % S5_PERM_STEPS, r % SUBLANES == c // S5_PERM_STEPS).astype(BF16)
    to_bt = jnp.logical_and(c // SUBLANES == r % S5_PERM_STEPS, c % SUBLANES == r // S5_PERM_STEPS).astype(BF16)
    u = jnp.concatenate(
        [jnp.dot(to_tb, u_ref[:, t0:t0 + S5_PERM_STEPS, :].reshape(prow, S5_WIDTH), preferred_element_type=F32)
         for t0 in range(0, steps, S5_PERM_STEPS)], axis=0)
    ub = u.astype(BF16)
    half_ch = S5_WIDTH // 2
    tiles_per_half = STATE_HALF // COL_TILE
    parts = []
    for kt in range(2):
        bu = jnp.dot(ub[:, kt * half_ch:(kt + 1) * half_ch], bw_ref[kt], preferred_element_type=F32)
        cols = []
        for cl in range(tiles_per_half):
            c = kt * tiles_per_half + cl
            ar = jnp.broadcast_to(ar_ref[:, c * LANES:(c + 1) * LANES], (SUBLANES, LANES))
            ai = jnp.broadcast_to(ai_ref[:, c * LANES:(c + 1) * LANES], (SUBLANES, LANES))
            xr = st_ref[:, c * COL_TILE:c * COL_TILE + LANES]
            xi = st_ref[:, c * COL_TILE + LANES:(c + 1) * COL_TILE]
            res_r, res_i = [], []
            for t in range(steps):
                rs = slice(t * SUBLANES, (t + 1) * SUBLANES)
                bur = bu[rs, cl * COL_TILE:cl * COL_TILE + LANES]
                bui = bu[rs, cl * COL_TILE + LANES:(cl + 1) * COL_TILE]
                xr, xi = ar * xr - ai * xi + bur, ar * xi + ai * xr + bui
                res_r.append(xr)
                res_i.append(xi)
            st_ref[:, c * COL_TILE:c * COL_TILE + LANES] = xr
            st_ref[:, c * COL_TILE + LANES:(c + 1) * COL_TILE] = xi
            cols += [jnp.concatenate(res_r, axis=0), jnp.concatenate(res_i, axis=0)]
        xh = jnp.concatenate(cols, axis=1).astype(BF16)
        parts.append(jnp.dot(xh, cw_ref[kt], preferred_element_type=F32))
    y = jnp.concatenate(parts, axis=1)
    y = y + d_ref[...] * u
    y = jax.nn.gelu(y)
    g = jnp.dot(y.astype(BF16), wglu_ref[...], preferred_element_type=F32)
    y = (y * jax.nn.sigmoid(g)).astype(BF16)
    for n, t0 in enumerate(range(0, steps, S5_PERM_STEPS)):
        y_bt = jnp.dot(to_bt, y[n * prow:(n + 1) * prow], preferred_element_type=F32).astype(BF16)
        y_ref[:, t0:t0 + S5_PERM_STEPS, :] = y_bt.reshape(SUBLANES, S5_PERM_STEPS, S5_WIDTH)


def _s5_params(lam_re, lam_im, log_dt, b_re, b_im, c_re, c_im):
    lr = lam_re.astype(F32)
    li = lam_im.astype(F32)
    dt = jnp.exp(log_dt.astype(F32))[:, None]
    mag = jnp.exp(lr * dt)
    abar_r = mag * jnp.cos(li * dt)
    abar_i = mag * jnp.sin(li * dt)
    den = lr * lr + li * li
    nr = abar_r - 1.0
    ni = abar_i
    fr = (nr * lr + ni * li) / den
    fi = (ni * lr - nr * li) / den
    br = b_re.astype(F32)
    bi = b_im.astype(F32)
    bbar_r = fr[..., None] * br - fi[..., None] * bi
    bbar_i = fr[..., None] * bi + fi[..., None] * br
    half_ch = S5_WIDTH // 2
    rp = 2 * S5_STATE
    a_b = jnp.stack([bbar_r, bbar_i], axis=1).transpose(0, 3, 1, 2).reshape(2, half_ch, rp)
    a_c = (jnp.stack([c_re.astype(F32), -c_im.astype(F32)], axis=1).transpose(1, 3, 0, 2)
           .reshape(rp, 2, half_ch).transpose(1, 0, 2))
    lane = jnp.arange(STATE_HALF)
    lane_rp = ((lane // LANES) % 2) * S5_STATE + lane % S5_STATE
    lane_group = 2 * (lane // COL_TILE) + (lane // S5_STATE) % 2
    sel = (jnp.arange(rp)[:, None] == lane_rp[None, :]).astype(F32)
    mask = (jnp.arange(half_ch) // S5_GROUP)[:, None] == lane_group[None, :]
    exact = lax.Precision.HIGHEST
    bw = jnp.where(mask, jnp.einsum('krc,cl->krl', a_b, sel, precision=exact), 0.0).astype(BF16)
    cw = jnp.where(mask.T, jnp.einsum('lc,kcr->klr', sel.T, a_c, precision=exact), 0.0).astype(BF16)
    a_r = abar_r.reshape(1, S5_GROUPS * S5_STATE)
    a_i = abar_i.reshape(1, S5_GROUPS * S5_STATE)
    return bw, cw, a_r, a_i


def _s5(u, lam_re, lam_im, log_dt, b_re, b_im, c_re, c_im, d, w_glu, steps=128):
    bsz, seqlen, _ = u.shape
    assert bsz == SUBLANES
    blk = pl.BlockSpec((bsz, steps, S5_WIDTH), lambda c: (0, c, 0))
    bw, cw, a_r, a_i = _s5_params(lam_re, lam_im, log_dt, b_re, b_im, c_re, c_im)
    return pl.pallas_call(
        functools.partial(_s5_kernel, steps=steps),
        grid=(seqlen // steps,),
        in_specs=[
            blk,
            _const_spec(bw.shape), _const_spec(cw.shape),
            _const_spec(a_r.shape), _const_spec(a_i.shape),
            _const_spec((1, S5_WIDTH)), _const_spec((S5_WIDTH, S5_WIDTH)),
        ],
        out_specs=blk,
        out_shape=jax.ShapeDtypeStruct(u.shape, BF16),
        scratch_shapes=[pltpu.VMEM((SUBLANES, STATE_LANES), F32)],
        compiler_params=pltpu.CompilerParams(
            dimension_semantics=("arbitrary",), vmem_limit_bytes=VMEM_LIMIT),
        name="s5",
    )(u, bw, cw, a_r, a_i, d.astype(F32)[None, :], w_glu.astype(BF16))


ATTN_GROUP_LANES = 256


def _attn_kernel(q_ref, k_ref, v_ref, o_ref, *, blk):
    i = pl.program_id(1)
    n_groups = q_ref.shape[-1] // ATTN_GROUP_LANES
    heads = ATTN_GROUP_LANES // SB_HEAD_DIM
    head_of_lane = lax.broadcasted_iota(jnp.int32, (blk, ATTN_GROUP_LANES), 1) // SB_HEAD_DIM
    row = lax.broadcasted_iota(jnp.int32, (blk, blk), 0)
    col = lax.broadcasted_iota(jnp.int32, (blk, blk), 1)
    from_key = (row >= col).astype(BF16)
    causal = col < row

    def lanes(g):
        return slice(g * ATTN_GROUP_LANES, (g + 1) * ATTN_GROUP_LANES)

    def load_kv(j, g):
        k0 = pl.multiple_of(j * blk, blk)
        return k_ref[0, pl.ds(k0, blk), lanes(g)], v_ref[0, pl.ds(k0, blk), lanes(g)]

    q_rows = []
    for g in range(n_groups):
        qg = q_ref[0, :, lanes(g)]
        q_rows.append(jnp.concatenate(
            [jnp.where(head_of_lane == h, qg, jnp.zeros_like(qg)) for h in range(heads)], axis=0))
    causal_rows = jnp.concatenate([causal] * heads, axis=0)

    def logits(q_all, kb, diag):
        z = lax.dot_general(q_all, kb, (((1,), (1,)), ((), ())), preferred_element_type=F32)
        neg_part = jnp.minimum(z, 0.0)
        neg_relu = neg_part - z
        log_1m = neg_relu - jnp.log2(1.0 + jnp.exp2(neg_part + neg_relu))
        if diag:
            log_1m = jnp.where(causal_rows, log_1m, 0.0)
        return z, log_1m

    def weighted_values(z, tail, vb, run, diag):
        expo = z + tail
        if run is not None:
            expo = expo + run
        w = jnp.exp2(expo)
        if diag:
            w = jnp.where(causal_rows, w, 0.0)
        pv = jnp.dot(w.astype(BF16), vb, preferred_element_type=F32)
        out = pv[(heads - 1) * blk:]
        for h in range(heads - 2, -1, -1):
            out = jnp.where(head_of_lane == h, pv[h * blk:(h + 1) * blk], out)
        return out, tail[:, :1]

    def group_tile(q_all, kb, vb, run, diag):
        z, log_1m = logits(q_all, kb, diag)
        tail = jnp.dot(log_1m.astype(BF16), from_key, preferred_element_type=F32)
        return weighted_values(z, tail, vb, run, diag)

    has_prev = (i > 0).astype(F32)
    rows = heads * blk
    parts = []
    for g in range(n_groups):
        kb0, vb0 = load_kv(i, g)
        kb1, vb1 = load_kv(jnp.maximum(i - 1, 0), g)
        parts += [(logits(q_rows[g], kb0, True), vb0), (logits(q_rows[g], kb1, False), vb1)]
    tails = jnp.dot(jnp.concatenate([lg[1] for lg, _ in parts], axis=0).astype(BF16), from_key,
                    preferred_element_type=F32)
    accs, runs = [], []
    for g in range(n_groups):
        (z0, _), vb0 = parts[2 * g]
        (z1, _), vb1 = parts[2 * g + 1]
        pv0, tot0 = weighted_values(z0, tails[(2 * g) * rows:(2 * g + 1) * rows], vb0, None, True)
        pv1, tot1 = weighted_values(z1, tails[(2 * g + 1) * rows:(2 * g + 2) * rows], vb1, tot0, False)
        accs.append(pv0 + has_prev * pv1)
        runs.append(tot0 + has_prev * tot1)

    def live(runs):
        top = runs[0]
        for r in runs[1:]:
            top = jnp.maximum(top, r)
        return jnp.max(top) > ZERO_WEIGHT_LOG2

    def cond(state):
        jj, alive = state[0], state[1]
        return jnp.logical_and(jj <= i, alive)

    def body(state):
        jj, accs, runs = state[0], state[2], state[3]
        new_accs, new_runs = [], []
        for g in range(n_groups):
            kb, vb = load_kv(i - jj, g)
            pv, tot = group_tile(q_rows[g], kb, vb, runs[g], False)
            new_accs.append(accs[g] + pv)
            new_runs.append(runs[g] + tot)
        return (jj + 1, live(new_runs), new_accs, new_runs)

    state = lax.while_loop(cond, body, (jnp.int32(2), live(runs), accs, runs))
    for g in range(n_groups):
        o_ref[0, :, lanes(g)] = state[2][g].astype(BF16)


def _sbattn(q, k, v, blk=256):
    bsz, seqlen, width = q.shape
    qspec = pl.BlockSpec((1, blk, width), lambda b, i: (b, i, 0))
    kvspec = pl.BlockSpec((1, seqlen, width), lambda b, i: (b, 0, 0))
    return pl.pallas_call(
        functools.partial(_attn_kernel, blk=blk),
        grid=(bsz, seqlen // blk),
        in_specs=[qspec, kvspec, kvspec],
        out_specs=qspec,
        out_shape=jax.ShapeDtypeStruct((bsz, seqlen, width), BF16),
        compiler_params=pltpu.CompilerParams(
            dimension_semantics=("parallel", "arbitrary"), vmem_limit_bytes=VMEM_LIMIT),
        name="sbattn",
    )(q, k, v)


FF_CHUNK = 1024


def _mlp_ple(h, p_ref, lnm_ref, wup_ref, wdn_ref, lnp_ref, wg_ref, wpu_ref, o_ref):
    hn = _rms(h, lnm_ref[...]).astype(BF16)
    acc = h
    for c in range(0, D_FF, FF_CHUNK):
        a = jnp.dot(hn, wup_ref[:, c:c + FF_CHUNK], preferred_element_type=F32)
        a = jnp.square(jnp.maximum(a, 0.0)).astype(BF16)
        acc = acc + jnp.dot(a, wdn_ref[c:c + FF_CHUNK, :], preferred_element_type=F32)
    gate = jax.nn.sigmoid(jnp.dot(_rms(acc, lnp_ref[...]).astype(BF16), wg_ref[...], preferred_element_type=F32))
    pe = jnp.dot(p_ref[0].astype(BF16), wpu_ref[...], preferred_element_type=F32)
    o_ref[0] = acc + pe * gate


def _layer_even_kernel(x_ref, s5_ref, sb_ref, wout_ref, p_ref, lnm_ref, wup_ref, wdn_ref, lnp_ref, wg_ref,
                       wpu_ref, o_ref):
    mixed = (jnp.dot(s5_ref[0], wout_ref[:S5_WIDTH, :], preferred_element_type=F32)
             + jnp.dot(sb_ref[0], wout_ref[S5_WIDTH:, :], preferred_element_type=F32))
    _mlp_ple(x_ref[0] + mixed, p_ref, lnm_ref, wup_ref, wdn_ref, lnp_ref, wg_ref, wpu_ref, o_ref)


def _layer_odd_kernel(h_ref, halo_ref, lno_ref, pw_ref, ps_ref, p_ref, lnm_ref, wup_ref, wdn_ref, lnp_ref,
                      wg_ref, wpu_ref, o_ref, *, tm):
    i = pl.program_id(1)
    h = h_ref[0]
    hn = _rms(h, lno_ref[...])
    halo = _rms(halo_ref[0], lno_ref[...]) * (i > 0).astype(F32)
    ext = jnp.concatenate([halo, hn], axis=0)
    t = lax.broadcasted_iota(jnp.int32, (tm, 1), 0) + i * tm
    outs = []
    for g, window in enumerate(POOL_WINDOWS):
        sl = slice(g * POOL_GROUP, (g + 1) * POOL_GROUP)
        s = ext[:, sl]
        span = 1
        while span < window:
            s = s + jnp.concatenate([jnp.zeros((span, POOL_GROUP), F32), s[:-span]], axis=0)
            span *= 2
        count = jnp.minimum(t + 1, window).astype(F32)
        y = s[POOL_HALO:] / count - hn[:, sl]
        outs.append(jnp.dot(y.astype(BF16), pw_ref[g], preferred_element_type=F32))
    mixed = jnp.concatenate(outs, axis=1) * ps_ref[...]
    _mlp_ple(h + mixed, p_ref, lnm_ref, wup_ref, wdn_ref, lnp_ref, wg_ref, wpu_ref, o_ref)


def _tail_specs(tm, layer):
    return [
        pl.BlockSpec((None, 1, tm, PLE_DIM), lambda b, i: (layer, b, i, 0)),
        _const_spec((1, D_MODEL)), _const_spec((D_MODEL, D_FF)), _const_spec((D_FF, D_MODEL)),
        _const_spec((1, D_MODEL)), _const_spec((D_MODEL, D_MODEL)), _const_spec((PLE_DIM, D_MODEL)),
    ]


def _tail_args(p, ln_mlp, w_up, w_down, ln_ple, w_gate, w_ple_up):
    return (p, ln_mlp.astype(F32)[None, :], w_up.astype(BF16), w_down.astype(BF16),
            ln_ple.astype(F32)[None, :], w_gate.astype(BF16), w_ple_up.astype(BF16))


def _layer_even(x, s5, sb, w_out, tail, layer, tm=1024):
    bsz, seqlen, _ = x.shape
    row = lambda w: pl.BlockSpec((1, tm, w), lambda b, i: (b, i, 0))
    return pl.pallas_call(
        _layer_even_kernel,
        grid=(bsz, seqlen // tm),
        in_specs=[row(D_MODEL),
                  row(S5_WIDTH),
                  row(SB_WIDTH),
                  _const_spec((D_MODEL, D_MODEL))] + _tail_specs(tm, layer),
        out_specs=row(D_MODEL),
        out_shape=jax.ShapeDtypeStruct(x.shape, F32),
        compiler_params=pltpu.CompilerParams(
            dimension_semantics=("parallel", "parallel"), vmem_limit_bytes=VMEM_LIMIT),
        name="layer_even",
    )(x, s5, sb, w_out.astype(BF16), *tail)


def _layer_odd(h, ln_odd, pool_w, pool_scale, tail, layer, tm=1024):
    bsz, seqlen, _ = h.shape
    row = lambda w: pl.BlockSpec((1, tm, w), lambda b, i: (b, i, 0))
    halo_blocks = tm // POOL_HALO
    return pl.pallas_call(
        functools.partial(_layer_odd_kernel, tm=tm),
        grid=(bsz, seqlen // tm),
        in_specs=[row(D_MODEL),
                  pl.BlockSpec((1, POOL_HALO, D_MODEL), lambda b, i: (b, jnp.maximum(i * halo_blocks - 1, 0), 0)),
                  _const_spec((1, D_MODEL)),
                  _const_spec((len(POOL_WINDOWS), POOL_GROUP, POOL_GROUP)),
                  _const_spec((1, D_MODEL))] + _tail_specs(tm, layer),
        out_specs=row(D_MODEL),
        out_shape=jax.ShapeDtypeStruct(h.shape, F32),
        compiler_params=pltpu.CompilerParams(
            dimension_semantics=("parallel", "parallel"), vmem_limit_bytes=VMEM_LIMIT),
        name="layer_odd",
    )(h, h, ln_odd.astype(F32)[None, :], pool_w.astype(BF16), pool_scale.astype(F32)[None, :], *tail)


def kernel(x, p, ln_mix_even, w_in_even, s5_lambda_re, s5_lambda_im, s5_log_dt, s5_b_re, s5_b_im, s5_c_re,
           s5_c_im, s5_d, s5_w_glu, sb_q_gain, sb_k_gain, w_out_even, ln_mix_odd, pool_w, pool_scale, ln_mlp,
           w_mlp_up, w_mlp_down, ln_ple, w_ple_gate, w_ple_up):
    h = x
    for i in range(p.shape[0]):
        j = i // 2
        tail = _tail_args(p, ln_mlp[i], w_mlp_up[i], w_mlp_down[i], ln_ple[i], w_ple_gate[i], w_ple_up[i])
        if i % 2 == 0:
            u, q, k, v = _inproj(h, ln_mix_even[j], w_in_even[j], sb_q_gain[j], sb_k_gain[j])
            s5 = _s5(u, s5_lambda_re[j], s5_lambda_im[j], s5_log_dt[j], s5_b_re[j], s5_b_im[j],
                     s5_c_re[j], s5_c_im[j], s5_d[j], s5_w_glu[j])
            sb = _sbattn(q, k, v)
            h = _layer_even(h, s5, sb, w_out_even[j], tail, i)
        else:
            h = _layer_odd(h, ln_mix_odd[j], pool_w[j], pool_scale[j], tail, i)
    return h
```

```python
import functools
import math

import jax
import jax.numpy as jnp
from jax import lax
from jax.experimental import pallas as pl
from jax.experimental.pallas import tpu as pltpu

F32 = jnp.float32
BF16 = jnp.bfloat16

D_MODEL = 1024
S5_WIDTH = 512
S5_GROUP = 16
S5_GROUPS = 32
S5_STATE = 64
SB_HEAD_DIM = 64
SB_WIDTH = 512
IN_WIDTH = S5_WIDTH + 3 * SB_WIDTH
POOL_WINDOWS = (2, 4, 8, 16)
POOL_GROUP = 256
POOL_HALO = max(POOL_WINDOWS)
D_FF = 4 * D_MODEL
PLE_DIM = 256
EPS = 1e-6
LOG2_E = math.log2(math.e)
ZERO_WEIGHT_LOG2 = -150.0

LANES = 128
SUBLANES = 8
STATE_LANES = 2 * S5_GROUPS * S5_STATE
STATE_HALF = STATE_LANES // 2
COL_TILE = 2 * LANES

V7X_VMEM_BYTES = 64 * 1024 * 1024
VMEM_LIMIT = V7X_VMEM_BYTES * 7 // 8


def _const_spec(shape):
    nd = len(shape)
    return pl.BlockSpec(shape, lambda *_: (0,) * nd, pipeline_mode=pl.Buffered(1))


def _rms(x, gain):
    ms = jnp.mean(x * x, axis=-1, keepdims=True)
    return x * lax.rsqrt(ms + EPS) * gain


def _inproj_kernel(x_ref, ln_ref, w_ref, qg_ref, kg_ref, hm_ref, u_ref, q_ref, k_ref, v_ref):
    hn = _rms(x_ref[0], ln_ref[...]).astype(BF16)
    proj = jnp.dot(hn, w_ref[...], preferred_element_type=F32)
    u_ref[0] = proj[:, :S5_WIDTH].astype(BF16)
    hm = hm_ref[...]

    def head_norm(t, gain):
        ms = jnp.dot((t * t).astype(BF16), hm, preferred_element_type=F32)
        return t * lax.rsqrt(ms + EPS) * gain

    q = proj[:, S5_WIDTH:S5_WIDTH + SB_WIDTH]
    k = proj[:, S5_WIDTH + SB_WIDTH:S5_WIDTH + 2 * SB_WIDTH]
    q_ref[0] = (head_norm(q, qg_ref[...]) * (SB_HEAD_DIM ** -0.5 * LOG2_E)).astype(BF16)
    k_ref[0] = head_norm(k, kg_ref[...]).astype(BF16)
    v_ref[0] = proj[:, S5_WIDTH + 2 * SB_WIDTH:].astype(BF16)


def _inproj(x, ln, w_in, q_gain, k_gain, tl=1024):
    bsz, seqlen, _ = x.shape
    heads = SB_WIDTH // SB_HEAD_DIM
    hm = jnp.kron(jnp.eye(heads, dtype=F32), jnp.full((SB_HEAD_DIM, SB_HEAD_DIM), 1.0 / SB_HEAD_DIM, F32)).astype(BF16)
    qg = jnp.tile(q_gain.astype(F32), heads)[None, :]
    kg = jnp.tile(k_gain.astype(F32), heads)[None, :]
    act = lambda: pl.BlockSpec((1, tl, SB_WIDTH), lambda b, i: (b, i, 0))
    return pl.pallas_call(
        _inproj_kernel,
        grid=(bsz, seqlen // tl),
        in_specs=[
            pl.BlockSpec((1, tl, D_MODEL), lambda b, i: (b, i, 0)),
            _const_spec((1, D_MODEL)),
            _const_spec((D_MODEL, IN_WIDTH)),
            _const_spec((1, SB_WIDTH)),
            _const_spec((1, SB_WIDTH)),
            _const_spec((SB_WIDTH, SB_WIDTH)),
        ],
        out_specs=[act(), act(), act(), act()],
        out_shape=[
            jax.ShapeDtypeStruct((bsz, seqlen, S5_WIDTH), BF16),
            jax.ShapeDtypeStruct((bsz, seqlen, SB_WIDTH), BF16),
            jax.ShapeDtypeStruct((bsz, seqlen, SB_WIDTH), BF16),
            jax.ShapeDtypeStruct((bsz, seqlen, SB_WIDTH), BF16),
        ],
        compiler_params=pltpu.CompilerParams(
            dimension_semantics=("parallel", "parallel"), vmem_limit_bytes=VMEM_LIMIT),
        name="inproj",
    )(x, ln.astype(F32)[None, :], w_in.astype(BF16), qg, kg, hm)


S5_PERM_STEPS = 32


def _s5_kernel(u_ref, bw_ref, cw_ref, ar_ref, ai_ref, d_ref, wglu_ref, y_ref, st_ref, *, steps):
    @pl.when(pl.program_id(0) == 0)
    def _():
        st_ref[...] = jnp.zeros_like(st_ref)

    prow = S5_PERM_STEPS * SUBLANES
    r = lax.broadcasted_iota(jnp.int32, (prow, prow), 0)
    c = lax.broadcasted_iota(jnp.int32, (prow, prow), 1)
    to_tb = jnp.logical_and(r // SUBLANES == c % S5_PERM_STEPS, r % SUBLANES == c // S5_PERM_STEPS).astype(BF16)
    to_bt = jnp.logical_and(c // SUBLANES == r % S5_PERM_STEPS, c % SUBLANES == r // S5_PERM_STEPS).astype(BF16)
    u = jnp.concatenate(
        [jnp.dot(to_tb, u_ref[:, t0:t0 + S5_PERM_STEPS, :].reshape(prow, S5_WIDTH), preferred_element_type=F32)
         for t0 in range(0, steps, S5_PERM_STEPS)], axis=0)
    ub = u.astype(BF16)
    half_ch = S5_WIDTH // 2
    tiles_per_half = STATE_HALF // COL_TILE
    parts = []
    for kt in range(2):
        bu = jnp.dot(ub[:, kt * half_ch:(kt + 1) * half_ch], bw_ref[kt], preferred_element_type=F32)
        cols = []
        for cl in range(tiles_per_half):
            c = kt * tiles_per_half + cl
            ar = jnp.broadcast_to(ar_ref[:, c * LANES:(c + 1) * LANES], (SUBLANES, LANES))
            ai = jnp.broadcast_to(ai_ref[:, c * LANES:(c + 1) * LANES], (SUBLANES, LANES))
            xr = st_ref[:, c * COL_TILE:c * COL_TILE + LANES]
            xi = st_ref[:, c * COL_TILE + LANES:(c + 1) * COL_TILE]
            res_r, res_i = [], []
            for t in range(steps):
                rs = slice(t * SUBLANES, (t + 1) * SUBLANES)
                bur = bu[rs, cl * COL_TILE:cl * COL_TILE + LANES]
                bui = bu[rs, cl * COL_TILE + LANES:(cl + 1) * COL_TILE]
                xr, xi = ar * xr - ai * xi + bur, ar * xi + ai * xr + bui
                res_r.append(xr)
                res_i.append(xi)
            st_ref[:, c * COL_TILE:c * COL_TILE + LANES] = xr
            st_ref[:, c * COL_TILE + LANES:(c + 1) * COL_TILE] = xi
            cols += [jnp.concatenate(res_r, axis=0), jnp.concatenate(res_i, axis=0)]
        xh = jnp.concatenate(cols, axis=1).astype(BF16)
        parts.append(jnp.dot(xh, cw_ref[kt], preferred_element_type=F32))
    y = jnp.concatenate(parts, axis=1)
    y = y + d_ref[...] * u
    y = jax.nn.gelu(y)
    g = jnp.dot(y.astype(BF16), wglu_ref[...], preferred_element_type=F32)
    y = (y * jax.nn.sigmoid(g)).astype(BF16)
    for n, t0 in enumerate(range(0, steps, S5_PERM_STEPS)):
        y_bt = jnp.dot(to_bt, y[n * prow:(n + 1) * prow], preferred_element_type=F32).astype(BF16)
        y_ref[:, t0:t0 + S5_PERM_STEPS, :] = y_bt.reshape(SUBLANES, S5_PERM_STEPS, S5_WIDTH)


def _s5_params(lam_re, lam_im, log_dt, b_re, b_im, c_re, c_im):
    lr = lam_re.astype(F32)
    li = lam_im.astype(F32)
    dt = jnp.exp(log_dt.astype(F32))[:, None]
    mag = jnp.exp(lr * dt)
    abar_r = mag * jnp.cos(li * dt)
    abar_i = mag * jnp.sin(li * dt)
    den = lr * lr + li * li
    nr = abar_r - 1.0
    ni = abar_i
    fr = (nr * lr + ni * li) / den
    fi = (ni * lr - nr * li) / den
    br = b_re.astype(F32)
    bi = b_im.astype(F32)
    bbar_r = fr[..., None] * br - fi[..., None] * bi
    bbar_i = fr[..., None] * bi + fi[..., None] * br
    half_ch = S5_WIDTH // 2
    rp = 2 * S5_STATE
    a_b = jnp.stack([bbar_r, bbar_i], axis=1).transpose(0, 3, 1, 2).reshape(2, half_ch, rp)
    a_c = (jnp.stack([c_re.astype(F32), -c_im.astype(F32)], axis=1).transpose(1, 3, 0, 2)
           .reshape(rp, 2, half_ch).transpose(1, 0, 2))
    lane = jnp.arange(STATE_HALF)
    lane_rp = ((lane // LANES) % 2) * S5_STATE + lane % S5_STATE
    lane_group = 2 * (lane // COL_TILE) + (lane // S5_STATE) % 2
    sel = (jnp.arange(rp)[:, None] == lane_rp[None, :]).astype(F32)
    mask = (jnp.arange(half_ch) // S5_GROUP)[:, None] == lane_group[None, :]
    exact = lax.Precision.HIGHEST
    bw = jnp.where(mask, jnp.einsum('krc,cl->krl', a_b, sel, precision=exact), 0.0).astype(BF16)
    cw = jnp.where(mask.T, jnp.einsum('lc,kcr->klr', sel.T, a_c, precision=exact), 0.0).astype(BF16)
    a_r = abar_r.reshape(1, S5_GROUPS * S5_STATE)
    a_i = abar_i.reshape(1, S5_GROUPS * S5_STATE)
    return bw, cw, a_r, a_i


def _s5(u, lam_re, lam_im, log_dt, b_re, b_im, c_re, c_im, d, w_glu, steps=128):
    bsz, seqlen, _ = u.shape
    assert bsz == SUBLANES
    blk = pl.BlockSpec((bsz, steps, S5_WIDTH), lambda c: (0, c, 0))
    bw, cw, a_r, a_i = _s5_params(lam_re, lam_im, log_dt, b_re, b_im, c_re, c_im)
    return pl.pallas_call(
        functools.partial(_s5_kernel, steps=steps),
        grid=(seqlen // steps,),
        in_specs=[
            blk,
            _const_spec(bw.shape), _const_spec(cw.shape),
            _const_spec(a_r.shape), _const_spec(a_i.shape),
            _const_spec((1, S5_WIDTH)), _const_spec((S5_WIDTH, S5_WIDTH)),
        ],
        out_specs=blk,
        out_shape=jax.ShapeDtypeStruct(u.shape, BF16),
        scratch_shapes=[pltpu.VMEM((SUBLANES, STATE_LANES), F32)],
        compiler_params=pltpu.CompilerParams(
            dimension_semantics=("arbitrary",), vmem_limit_bytes=VMEM_LIMIT),
        name="s5",
    )(u, bw, cw, a_r, a_i, d.astype(F32)[None, :], w_glu.astype(BF16))


ATTN_GROUP_LANES = 256
MASKED_LOGIT = -1e30


def _attn_kernel(q_ref, k_ref, v_ref, o_ref, *, blk):
    i = pl.program_id(1)
    n_groups = q_ref.shape[-1] // ATTN_GROUP_LANES
    heads = ATTN_GROUP_LANES // SB_HEAD_DIM
    head_of_lane = lax.broadcasted_iota(jnp.int32, (blk, ATTN_GROUP_LANES), 1) // SB_HEAD_DIM
    row = lax.broadcasted_iota(jnp.int32, (blk, blk), 0)
    col = lax.broadcasted_iota(jnp.int32, (blk, blk), 1)
    from_key = (row >= col).astype(BF16)
    causal = col < row

    def lanes(g):
        return slice(g * ATTN_GROUP_LANES, (g + 1) * ATTN_GROUP_LANES)

    def load_kv(j, g):
        k0 = pl.multiple_of(j * blk, blk)
        return k_ref[0, pl.ds(k0, blk), lanes(g)], v_ref[0, pl.ds(k0, blk), lanes(g)]

    q_rows = []
    for g in range(n_groups):
        qg = q_ref[0, :, lanes(g)]
        q_rows.append(jnp.concatenate(
            [jnp.where(head_of_lane == h, qg, jnp.zeros_like(qg)) for h in range(heads)], axis=0))
    causal_rows = jnp.concatenate([causal] * heads, axis=0)

    def logits(q_all, kb, diag):
        z = lax.dot_general(q_all, kb, (((1,), (1,)), ((), ())), preferred_element_type=F32)
        if diag:
            z = jnp.where(causal_rows, z, MASKED_LOGIT)
        neg_part = jnp.minimum(z, 0.0)
        neg_relu = neg_part - z
        log_1m = neg_relu - jnp.log2(1.0 + jnp.exp2(neg_part + neg_relu))
        return z, log_1m

    def weighted_values(z, tail, vb, run):
        expo = z + tail
        if run is not None:
            expo = expo + run
        w = jnp.exp2(expo)
        pv = jnp.dot(w.astype(BF16), vb, preferred_element_type=F32)
        out = pv[(heads - 1) * blk:]
        for h in range(heads - 2, -1, -1):
            out = jnp.where(head_of_lane == h, pv[h * blk:(h + 1) * blk], out)
        return out, tail[:, :1]

    def group_tile(q_all, kb, vb, run, diag):
        z, log_1m = logits(q_all, kb, diag)
        tail = jnp.dot(log_1m.astype(BF16), from_key, preferred_element_type=F32)
        return weighted_values(z, tail, vb, run)

    has_prev = (i > 0).astype(F32)
    rows = heads * blk
    parts = []
    for g in range(n_groups):
        kb0, vb0 = load_kv(i, g)
        kb1, vb1 = load_kv(jnp.maximum(i - 1, 0), g)
        parts += [(logits(q_rows[g], kb0, True), vb0), (logits(q_rows[g], kb1, False), vb1)]
    tails = jnp.dot(jnp.concatenate([lg[1] for lg, _ in parts], axis=0).astype(BF16), from_key,
                    preferred_element_type=F32)
    accs, runs = [], []
    for g in range(n_groups):
        (z0, _), vb0 = parts[2 * g]
        (z1, _), vb1 = parts[2 * g + 1]
        pv0, tot0 = weighted_values(z0, tails[(2 * g) * rows:(2 * g + 1) * rows], vb0, None)
        pv1, tot1 = weighted_values(z1, tails[(2 * g + 1) * rows:(2 * g + 2) * rows], vb1, tot0)
        accs.append(pv0 + has_prev * pv1)
        runs.append(tot0 + has_prev * tot1)

    def live(runs):
        top = runs[0]
        for r in runs[1:]:
            top = jnp.maximum(top, r)
        return jnp.max(top) > ZERO_WEIGHT_LOG2

    def cond(state):
        jj, alive = state[0], state[1]
        return jnp.logical_and(jj <= i, alive)

    def body(state):
        jj, accs, runs = state[0], state[2], state[3]
        new_accs, new_runs = [], []
        for g in range(n_groups):
            kb, vb = load_kv(i - jj, g)
            pv, tot = group_tile(q_rows[g], kb, vb, runs[g], False)
            new_accs.append(accs[g] + pv)
            new_runs.append(runs[g] + tot)
        return (jj + 1, live(new_runs), new_accs, new_runs)

    state = lax.while_loop(cond, body, (jnp.int32(2), live(runs), accs, runs))
    for g in range(n_groups):
        o_ref[0, :, lanes(g)] = state[2][g].astype(BF16)


def _sbattn(q, k, v, blk=256):
    bsz, seqlen, width = q.shape
    qspec = pl.BlockSpec((1, blk, width), lambda b, i: (b, i, 0))
    kvspec = pl.BlockSpec((1, seqlen, width), lambda b, i: (b, 0, 0))
    return pl.pallas_call(
        functools.partial(_attn_kernel, blk=blk),
        grid=(bsz, seqlen // blk),
        in_specs=[qspec, kvspec, kvspec],
        out_specs=qspec,
        out_shape=jax.ShapeDtypeStruct((bsz, seqlen, width), BF16),
        compiler_params=pltpu.CompilerParams(
            dimension_semantics=("parallel", "arbitrary"), vmem_limit_bytes=VMEM_LIMIT),
        name="sbattn",
    )(q, k, v)


FF_CHUNK = 1024


def _mlp_ple(h, p_ref, lnm_ref, wup_ref, wdn_ref, lnp_ref, wg_ref, wpu_ref, o_ref):
    hn = _rms(h, lnm_ref[...]).astype(BF16)
    acc = h
    for c in range(0, D_FF, FF_CHUNK):
        a = jnp.dot(hn, wup_ref[:, c:c + FF_CHUNK], preferred_element_type=F32)
        a = jnp.square(jnp.maximum(a, 0.0)).astype(BF16)
        acc = acc + jnp.dot(a, wdn_ref[c:c + FF_CHUNK, :], preferred_element_type=F32)
    gate = jax.nn.sigmoid(jnp.dot(_rms(acc, lnp_ref[...]).astype(BF16), wg_ref[...], preferred_element_type=F32))
    pe = jnp.dot(p_ref[0].astype(BF16), wpu_ref[...], preferred_element_type=F32)
    o_ref[0] = acc + pe * gate


def _layer_even_kernel(x_ref, s5_ref, sb_ref, wout_ref, p_ref, lnm_ref, wup_ref, wdn_ref, lnp_ref, wg_ref,
                       wpu_ref, o_ref):
    mixed = (jnp.dot(s5_ref[0], wout_ref[:S5_WIDTH, :], preferred_element_type=F32)
             + jnp.dot(sb_ref[0], wout_ref[S5_WIDTH:, :], preferred_element_type=F32))
    _mlp_ple(x_ref[0] + mixed, p_ref, lnm_ref, wup_ref, wdn_ref, lnp_ref, wg_ref, wpu_ref, o_ref)


def _layer_odd_kernel(h_ref, halo_ref, lno_ref, pw_ref, ps_ref, p_ref, lnm_ref, wup_ref, wdn_ref, lnp_ref,
                      wg_ref, wpu_ref, o_ref, *, tm):
    i = pl.program_id(1)
    h = h_ref[0]
    hn = _rms(h, lno_ref[...])
    halo = _rms(halo_ref[0], lno_ref[...]) * (i > 0).astype(F32)
    ext = jnp.concatenate([halo, hn], axis=0)
    t = lax.broadcasted_iota(jnp.int32, (tm, 1), 0) + i * tm
    outs = []
    for g, window in enumerate(POOL_WINDOWS):
        sl = slice(g * POOL_GROUP, (g + 1) * POOL_GROUP)
        s = ext[:, sl]
        span = 1
        while span < window:
            s = s + jnp.concatenate([jnp.zeros((span, POOL_GROUP), F32), s[:-span]], axis=0)
            span *= 2
        count = jnp.minimum(t + 1, window).astype(F32)
        y = s[POOL_HALO:] / count - hn[:, sl]
        outs.append(jnp.dot(y.astype(BF16), pw_ref[g], preferred_element_type=F32))
    mixed = jnp.concatenate(outs, axis=1) * ps_ref[...]
    _mlp_ple(h + mixed, p_ref, lnm_ref, wup_ref, wdn_ref, lnp_ref, wg_ref, wpu_ref, o_ref)


def _tail_specs(tm, layer):
    return [
        pl.BlockSpec((None, 1, tm, PLE_DIM), lambda b, i: (layer, b, i, 0)),
        _const_spec((1, D_MODEL)), _const_spec((D_MODEL, D_FF)), _const_spec((D_FF, D_MODEL)),
        _const_spec((1, D_MODEL)), _const_spec((D_MODEL, D_MODEL)), _const_spec((PLE_DIM, D_MODEL)),
    ]


def _tail_args(p, ln_mlp, w_up, w_down, ln_ple, w_gate, w_ple_up):
    return (p, ln_mlp.astype(F32)[None, :], w_up.astype(BF16), w_down.astype(BF16),
            ln_ple.astype(F32)[None, :], w_gate.astype(BF16), w_ple_up.astype(BF16))


def _layer_even(x, s5, sb, w_out, tail, layer, tm=1024):
    bsz, seqlen, _ = x.shape
    row = lambda w: pl.BlockSpec((1, tm, w), lambda b, i: (b, i, 0))
    return pl.pallas_call(
        _layer_even_kernel,
        grid=(bsz, seqlen // tm),
        in_specs=[row(D_MODEL),
                  row(S5_WIDTH),
                  row(SB_WIDTH),
                  _const_spec((D_MODEL, D_MODEL))] + _tail_specs(tm, layer),
        out_specs=row(D_MODEL),
        out_shape=jax.ShapeDtypeStruct(x.shape, F32),
        compiler_params=pltpu.CompilerParams(
            dimension_semantics=("parallel", "parallel"), vmem_limit_bytes=VMEM_LIMIT),
        name="layer_even",
    )(x, s5, sb, w_out.astype(BF16), *tail)


def _layer_odd(h, ln_odd, pool_w, pool_scale, tail, layer, tm=1024):
    bsz, seqlen, _ = h.shape
    row = lambda w: pl.BlockSpec((1, tm, w), lambda b, i: (b, i, 0))
    halo_blocks = tm // POOL_HALO
    return pl.pallas_call(
        functools.partial(_layer_odd_kernel, tm=tm),
        grid=(bsz, seqlen // tm),
        in_specs=[row(D_MODEL),
                  pl.BlockSpec((1, POOL_HALO, D_MODEL), lambda b, i: (b, jnp.maximum(i * halo_blocks - 1, 0), 0)),
                  _const_spec((1, D_MODEL)),
                  _const_spec((len(POOL_WINDOWS), POOL_GROUP, POOL_GROUP)),
                  _const_spec((1, D_MODEL))] + _tail_specs(tm, layer),
        out_specs=row(D_MODEL),
        out_shape=jax.ShapeDtypeStruct(h.shape, F32),
        compiler_params=pltpu.CompilerParams(
            dimension_semantics=("parallel", "parallel"), vmem_limit_bytes=VMEM_LIMIT),
        name="layer_odd",
    )(h, h, ln_odd.astype(F32)[None, :], pool_w.astype(BF16), pool_scale.astype(F32)[None, :], *tail)


def kernel(x, p, ln_mix_even, w_in_even, s5_lambda_re, s5_lambda_im, s5_log_dt, s5_b_re, s5_b_im, s5_c_re,
           s5_c_im, s5_d, s5_w_glu, sb_q_gain, sb_k_gain, w_out_even, ln_mix_odd, pool_w, pool_scale, ln_mlp,
           w_mlp_up, w_mlp_down, ln_ple, w_ple_gate, w_ple_up):
    h = x
    for i in range(p.shape[0]):
        j = i // 2
        tail = _tail_args(p, ln_mlp[i], w_mlp_up[i], w_mlp_down[i], ln_ple[i], w_ple_gate[i], w_ple_up[i])
        if i % 2 == 0:
            u, q, k, v = _inproj(h, ln_mix_even[j], w_in_even[j], sb_q_gain[j], sb_k_gain[j])
            s5 = _s5(u, s5_lambda_re[j], s5_lambda_im[j], s5_log_dt[j], s5_b_re[j], s5_b_im[j],
                     s5_c_re[j], s5_c_im[j], s5_d[j], s5_w_glu[j])
            sb = _sbattn(q, k, v)
            h = _layer_even(h, s5, sb, w_out_even[j], tail, i)
        else:
            h = _layer_odd(h, ln_mix_odd[j], pool_w[j], pool_scale[j], tail, i)
    return h
```
